```python
import jax, jax.numpy as jnp
from jax import lax
import numpy as np

D_MODEL = 1024
BATCH = 2
SEQ = 8192
DEPTH = 1

GRID_W = 64
CTX_LEN = 256
HEAD_DIM = 64
N_Q_HEADS = 8
N_KV_HEADS = 2
GQA_GROUP = N_Q_HEADS // N_KV_HEADS
WINDOW = 128
BLOCK = 128
ROPE_BASE = 10000.0
ROPE_PAIRS = HEAD_DIM // 4
N_GMLP_GROUPS = 8
GMLP_GROUP_DIM = 64
GMLP_WIDTH = N_GMLP_GROUPS * GMLP_GROUP_DIM
CHUNK = 128
FFN_HIDDEN = ((8 * D_MODEL // 3 + 255) // 256) * 256
Q_W = N_Q_HEADS * HEAD_DIM
KV_W = N_KV_HEADS * HEAD_DIM
IN_SPLITS = (Q_W, Q_W + KV_W, Q_W + 2 * KV_W, Q_W + 2 * KV_W + GMLP_WIDTH,
             Q_W + 2 * KV_W + 2 * GMLP_WIDTH, Q_W + 2 * KV_W + 2 * GMLP_WIDTH + D_MODEL)
IN_W = Q_W + 2 * KV_W + 2 * GMLP_WIDTH + 2 * D_MODEL
LN_EPS = 1e-5
NEG_INF = -1e30
DEEPNORM_ALPHA = (2 * DEPTH) ** 0.25
DEEPNORM_BETA = (8 * DEPTH) ** -0.25

kernel_name = 'hybrid_window_gqa_gmlp_dit_block'


def layer_norm(x, g=None, b=None):
    xf = x.astype(jnp.float32)
    mu = jnp.mean(xf, axis=-1, keepdims=True)
    var = jnp.mean(jnp.square(xf - mu), axis=-1, keepdims=True)
    y = (xf - mu) * lax.rsqrt(var + LN_EPS)
    if g is not None:
        y = y * g.astype(jnp.float32) + b.astype(jnp.float32)
    return y.astype(x.dtype)


def modulate(y, shift, scale):
    return y * (1 + scale[..., None, :]) + shift[..., None, :]


def axial_rope(t, rows, cols):
    inv = ROPE_BASE ** (-jnp.arange(ROPE_PAIRS, dtype=jnp.float32) / ROPE_PAIRS)

    def rot(xa, pos):
        ang = pos.astype(jnp.float32)[:, None] * inv
        cos = jnp.cos(ang)[:, None, :].astype(t.dtype)
        sin = jnp.sin(ang)[:, None, :].astype(t.dtype)
        x1, x2 = xa[..., :ROPE_PAIRS], xa[..., ROPE_PAIRS:]
        return jnp.concatenate([x1 * cos - x2 * sin, x1 * sin + x2 * cos], axis=-1)

    half = HEAD_DIM // 2
    return jnp.concatenate([rot(t[..., :half], rows), rot(t[..., half:], cols)], axis=-1)


def window_attention(q, k, v, kc, vc, sink):
    B, L = q.shape[:2]
    nb = L // BLOCK
    C = kc.shape[1]
    qb = q.reshape(B, nb, BLOCK, N_KV_HEADS, GQA_GROUP, HEAD_DIM)

    def band(t):
        tp = jnp.pad(t, ((0, 0), (BLOCK, BLOCK), (0, 0), (0, 0)))
        tp = tp.reshape(B, nb + 2, BLOCK, N_KV_HEADS, HEAD_DIM)
        return jnp.concatenate([tp[:, :-2], tp[:, 1:-1], tp[:, 2:]], axis=2)

    kw, vw = band(k), band(v)
    scale = HEAD_DIM ** -0.5
    s_loc = jnp.einsum('bnqhgd,bnjhd->bnhgqj', qb, kw).astype(jnp.float32) * scale
    s_ctx = jnp.einsum('bnqhgd,bchd->bnhgqc', qb, kc).astype(jnp.float32) * scale
    qi = jnp.arange(BLOCK)[:, None]
    kj = jnp.arange(3 * BLOCK)[None, :]
    rel = kj - BLOCK - qi
    kpos = jnp.arange(nb)[:, None, None] * BLOCK - BLOCK + kj[None]
    valid = (jnp.abs(rel) <= WINDOW)[None] & (kpos >= 0) & (kpos < L)
    s_loc = jnp.where(valid[None, :, None, None], s_loc, NEG_INF)
    sink_l = jnp.broadcast_to(
        sink.astype(jnp.float32).reshape(N_KV_HEADS, GQA_GROUP)[None, None, :, :, None, None],
        s_loc.shape[:-1] + (1,))
    logits = jnp.concatenate([sink_l, s_ctx, s_loc], axis=-1)
    probs = jax.nn.softmax(logits, axis=-1).astype(v.dtype)
    p_ctx, p_loc = probs[..., 1:1 + C], probs[..., 1 + C:]
    out = (jnp.einsum('bnhgqc,bchd->bnqhgd', p_ctx, vc)
           + jnp.einsum('bnhgqj,bnjhd->bnqhgd', p_loc, vw))
    return out.reshape(B, L, Q_W)


def context_attention(q, k, v, sink):
    B, C = q.shape[:2]
    qg = q.reshape(B, C, N_KV_HEADS, GQA_GROUP, HEAD_DIM)
    s = jnp.einsum('bqhgd,bkhd->bhgqk', qg, k).astype(jnp.float32) * HEAD_DIM ** -0.5
    sink_l = jnp.broadcast_to(
        sink.astype(jnp.float32).reshape(N_KV_HEADS, GQA_GROUP)[None, :, :, None, None],
        s.shape[:-1] + (1,))
    probs = jax.nn.softmax(jnp.concatenate([sink_l, s], axis=-1), axis=-1).astype(v.dtype)
    out = jnp.einsum('bhgqk,bkhd->bqhgd', probs[..., 1:], v)
    return out.reshape(B, C, Q_W)


def chunk_gmlp(u, vb, ln_g, ln_b, w_s, b_s):
    B, L = u.shape[:2]
    nc = L // CHUNK
    vn = layer_norm(vb, ln_g, ln_b).reshape(B, nc, CHUNK, N_GMLP_GROUPS, GMLP_GROUP_DIM)
    s = jnp.einsum('gij,bnjgd->bnigd', w_s, vn) + b_s.T[None, None, :, :, None]
    return u * s.reshape(B, L, GMLP_WIDTH)


def token_mixer(h, w_in, sink, gmlp_g, gmlp_b, w_s, b_s, w_a, w_b, w_o, kc, vc, pos):
    B, L = h.shape[:2]
    q, k, v, u, vb, ga, gb = jnp.split(h @ w_in, IN_SPLITS, axis=-1)
    q = q.reshape(B, L, N_Q_HEADS, HEAD_DIM)
    k = k.reshape(B, L, N_KV_HEADS, HEAD_DIM)
    v = v.reshape(B, L, N_KV_HEADS, HEAD_DIM)
    if pos is None:
        ya = context_attention(q, k, v, sink)
    else:
        rows, cols = pos
        ya = window_attention(axial_rope(q, rows, cols), axial_rope(k, rows, cols), v, kc, vc, sink)
    yb = chunk_gmlp(jax.nn.gelu(u), jax.nn.gelu(vb), gmlp_g, gmlp_b, w_s, b_s)
    merged = jax.nn.sigmoid(ga) * (ya @ w_a) + jax.nn.sigmoid(gb) * (yb @ w_b)
    return merged @ w_o


def context_kv(hc, w_in):
    B, C = hc.shape[:2]
    k, v = jnp.split(hc @ w_in[:, Q_W:Q_W + 2 * KV_W], 2, axis=-1)
    return (k.reshape(B, C, N_KV_HEADS, HEAD_DIM), v.reshape(B, C, N_KV_HEADS, HEAD_DIM))


def swiglu(h, w_ffn_in, w_ffn_out):
    gate, up = jnp.split(h @ w_ffn_in, 2, axis=-1)
    return (jax.nn.silu(gate) * up) @ w_ffn_out


def post_norm(res, out, g, b):
    return layer_norm(DEEPNORM_ALPHA * res + out, g, b)


def setup_inputs(seed: int = 0) -> dict:
    key = jax.random.key(seed)
    ks = jax.random.split(key, 21)
    f32 = jnp.float32

    def nrm(k, shape, s):
        return jax.random.normal(k, shape, f32) * s

    return {
        'x': nrm(ks[0], (BATCH, SEQ, D_MODEL), 1.0),
        'c': nrm(ks[1], (BATCH, D_MODEL), 1.0),
        'ctx': nrm(ks[2], (BATCH, CTX_LEN, D_MODEL), 1.0),
        'c_ctx': nrm(ks[3], (D_MODEL,), 1.0),
        'w_ada': nrm(ks[4], (DEPTH, D_MODEL, 6 * D_MODEL), 0.5 * D_MODEL ** -0.5),
        'b_ada': nrm(ks[5], (DEPTH, 6 * D_MODEL), 0.02),
        'w_in': nrm(ks[6], (DEPTH, D_MODEL, IN_W), D_MODEL ** -0.5),
        'attn_sink': nrm(ks[7], (DEPTH, N_Q_HEADS), 0.5),
        'gmlp_ln_g': 1.0 + nrm(ks[8], (DEPTH, GMLP_WIDTH), 0.02),
        'gmlp_ln_b': nrm(ks[9], (DEPTH, GMLP_WIDTH), 0.02),
        'w_spatial': nrm(ks[10], (DEPTH, N_GMLP_GROUPS, CHUNK, CHUNK), CHUNK ** -0.5),
        'b_spatial': 1.0 + nrm(ks[11], (DEPTH, N_GMLP_GROUPS, CHUNK), 0.02),
        'w_branch_a': nrm(ks[12], (DEPTH, Q_W, D_MODEL), Q_W ** -0.5),
        'w_branch_b': nrm(ks[13], (DEPTH, GMLP_WIDTH, D_MODEL), GMLP_WIDTH ** -0.5),
        'w_out': nrm(ks[14], (DEPTH, D_MODEL, D_MODEL), DEEPNORM_BETA * D_MODEL ** -0.5),
        'ln1_g': 1.0 + nrm(ks[15], (DEPTH, D_MODEL), 0.02),
        'ln1_b': nrm(ks[16], (DEPTH, D_MODEL), 0.02),
        'w_ffn_in': nrm(ks[17], (DEPTH, D_MODEL, 2 * FFN_HIDDEN), D_MODEL ** -0.5),
        'w_ffn_out': nrm(ks[18], (DEPTH, FFN_HIDDEN, D_MODEL), DEEPNORM_BETA * FFN_HIDDEN ** -0.5),
        'ln2_g': 1.0 + nrm(ks[19], (DEPTH, D_MODEL), 0.02),
        'ln2_b': nrm(ks[20], (DEPTH, D_MODEL), 0.02),
    }


def reference(x, c, ctx, c_ctx, w_ada, b_ada, w_in, attn_sink, gmlp_ln_g, gmlp_ln_b,
              w_spatial, b_spatial, w_branch_a, w_branch_b, w_out, ln1_g, ln1_b,
              w_ffn_in, w_ffn_out, ln2_g, ln2_b):
    L = x.shape[1]
    n_rows = L // GRID_W
    rows = jnp.repeat(jnp.arange(n_rows, dtype=jnp.int32), GRID_W)
    cols = jnp.tile(jnp.arange(GRID_W, dtype=jnp.int32), n_rows)

    for layer in range(DEPTH):
        mod_x = jnp.split(jax.nn.silu(c) @ w_ada[layer] + b_ada[layer], 6, axis=-1)
        mod_c = jnp.split(jax.nn.silu(c_ctx) @ w_ada[layer] + b_ada[layer], 6, axis=-1)
        mix_params = (w_in[layer], attn_sink[layer], gmlp_ln_g[layer], gmlp_ln_b[layer],
                      w_spatial[layer], b_spatial[layer], w_branch_a[layer], w_branch_b[layer],
                      w_out[layer])

        hc = modulate(layer_norm(ctx), mod_c[0], mod_c[1])
        kc, vc = context_kv(hc, w_in[layer])

        h = modulate(layer_norm(x), mod_x[0], mod_x[1])
        mix = token_mixer(h, *mix_params, kc, vc, (rows, cols))
        x_mid = post_norm(x, mod_x[2][:, None, :] * mix, ln1_g[layer], ln1_b[layer])
        h2 = modulate(layer_norm(x_mid), mod_x[3], mod_x[4])
        x_new = post_norm(x_mid, mod_x[5][:, None, :] * swiglu(h2, w_ffn_in[layer], w_ffn_out[layer]),
                          ln2_g[layer], ln2_b[layer])

        if layer < DEPTH - 1:
            mix_c = token_mixer(hc, *mix_params, None, None, None)
            ctx_mid = post_norm(ctx, mod_c[2] * mix_c, ln1_g[layer], ln1_b[layer])
            h2c = modulate(layer_norm(ctx_mid), mod_c[3], mod_c[4])
            ctx = post_norm(ctx_mid, mod_c[5] * swiglu(h2c, w_ffn_in[layer], w_ffn_out[layer]),
                            ln2_g[layer], ln2_b[layer])
        x = x_new
    return x
```

```python
import functools

import jax
import jax.numpy as jnp
from jax import lax
from jax.experimental import pallas as pl
from jax.experimental.pallas import tpu as pltpu

F32 = jnp.float32
BF16 = jnp.bfloat16

D_MODEL = 1024
GRID_W = 64
HEAD_DIM = 64
N_Q_HEADS = 8
N_KV_HEADS = 2
GQA_GROUP = N_Q_HEADS // N_KV_HEADS
BLOCK = 128
ROPE_BASE = 10000.0
ROPE_PAIRS = HEAD_DIM // 4
N_GMLP_GROUPS = 8
GMLP_GROUP_DIM = 64
GMLP_WIDTH = N_GMLP_GROUPS * GMLP_GROUP_DIM
FFN_HIDDEN = 2816
Q_W = N_Q_HEADS * HEAD_DIM
KV_W = N_KV_HEADS * HEAD_DIM
OFF_Q, OFF_K, OFF_U = 0, Q_W, Q_W + 2 * KV_W
OFF_VB = OFF_U + GMLP_WIDTH
OFF_GA = OFF_VB + GMLP_WIDTH
OFF_GB = OFF_GA + D_MODEL
IN_W = OFF_GB + D_MODEL
LN_EPS = 1e-5
NEG_INF = -1e30
DEPTH = 1
DEEPNORM_ALPHA = (2 * DEPTH) ** 0.25
ATTN_SCALE = HEAD_DIM ** -0.5

LANES = 128
ADA_BLOCK_N = 1536
KV_TILE = 512
MIX_TILE = 512
FFN_TILE = 512
FFN_CHUNK = 256
VMEM_LIMIT = 56 * 1024 * 1024


def _layer_norm(x):
    mu = jnp.mean(x, axis=-1, keepdims=True)
    xc = x - mu
    var = jnp.mean(xc * xc, axis=-1, keepdims=True)
    return xc * lax.rsqrt(var + LN_EPS)


def _gelu_tanh(x):
    return 0.5 * x * (1.0 + jnp.tanh(0.7978845608028654 * (x + 0.044715 * (x * x * x))))


def _rope(t, cos, sin_signed):
    lane = lax.broadcasted_iota(jnp.int32, t.shape, 1)
    partner = jnp.where((lane & ROPE_PAIRS) == 0,
                        pltpu.roll(t, LANES - ROPE_PAIRS, 1), pltpu.roll(t, ROPE_PAIRS, 1))
    return t * cos + partner * sin_signed


def _resident(shape):
    zeros = (0,) * len(shape)
    return pl.BlockSpec(shape, lambda *_: zeros, pipeline_mode=pl.Buffered(1))


def _ada_kernel(c_ref, w_ref, b_ref, o_ref):
    c = c_ref[...]
    s = (c * jax.nn.sigmoid(c)).astype(BF16)
    o_ref[...] = jnp.dot(s, w_ref[...].astype(BF16), preferred_element_type=F32) + b_ref[...]


def _ada_rows(c_rows, w_ada, b_ada):
    n = w_ada.shape[1]
    return pl.pallas_call(
        _ada_kernel,
        grid=(n // ADA_BLOCK_N,),
        in_specs=[pl.BlockSpec(c_rows.shape, lambda j: (0, 0)),
                  pl.BlockSpec((D_MODEL, ADA_BLOCK_N), lambda j: (0, j)),
                  pl.BlockSpec((1, ADA_BLOCK_N), lambda j: (0, j))],
        out_specs=pl.BlockSpec((c_rows.shape[0], ADA_BLOCK_N), lambda j: (0, j)),
        out_shape=jax.ShapeDtypeStruct((c_rows.shape[0], n), F32),
        name="ada_rows",
    )(c_rows, w_ada, b_ada.reshape(1, n))


def _kv_kernel(*refs, rope):
    if rope:
        x_ref, mod_ref, w_ref, cos_ref, sin_ref, k_ref, v_ref = refs
    else:
        x_ref, mod_ref, w_ref, k_ref, v_ref = refs
    m = mod_ref[0]
    h = _layer_norm(x_ref[0]) * (1.0 + m[1:2]) + m[0:1]
    kv = jnp.dot(h.astype(BF16), w_ref[...], preferred_element_type=F32)
    k, v = kv[:, :KV_W], kv[:, KV_W:]
    if rope:
        k = _rope(k, cos_ref[...], sin_ref[...])
    for g in range(N_KV_HEADS):
        k_ref[0, g] = k[:, g * HEAD_DIM:(g + 1) * HEAD_DIM].astype(BF16)
        v_ref[0, g] = v[:, g * HEAD_DIM:(g + 1) * HEAD_DIM].astype(BF16)


def _kv_project(x, mod, mod_row, w_in_b, tables, tile):
    bsz, t, _ = x.shape
    rope = tables is not None
    in_specs = [pl.BlockSpec((1, tile, D_MODEL), lambda b, i: (b, i, 0)),
                pl.BlockSpec((1, 6, D_MODEL), lambda b, i: (mod_row(b), 0, 0)),
                pl.BlockSpec((D_MODEL, 2 * KV_W), lambda b, i: (0, OFF_K // (2 * KV_W)))]
    args = [x, mod, w_in_b]
    if rope:
        in_specs += [pl.BlockSpec((tile, LANES), lambda b, i: (i, 0))] * 2
        args += list(tables)
    out_spec = pl.BlockSpec((1, N_KV_HEADS, tile, HEAD_DIM), lambda b, i: (b, 0, i, 0))
    out_shape = jax.ShapeDtypeStruct((bsz, N_KV_HEADS, t, HEAD_DIM), BF16)
    return pl.pallas_call(
        functools.partial(_kv_kernel, rope=rope),
        grid=(bsz, t // tile),
        in_specs=in_specs,
        out_specs=[out_spec, out_spec],
        out_shape=[out_shape, out_shape],
        name="kv_latent" if rope else "kv_context",
    )(*args)


def _attention_block(q_blk, kc, vc, k_loc, v_loc, sinks, lo, hi):
    n_ctx = kc.shape[0]
    kk = jnp.concatenate([kc, k_loc], axis=0)
    vv = jnp.concatenate([vc, v_loc], axis=0)
    s = lax.dot_general(q_blk, kk, (((1,), (1,)), ((), ())), preferred_element_type=F32)
    qi = lax.broadcasted_iota(jnp.int32, (BLOCK, n_ctx + 3 * BLOCK), 0)
    kj = lax.broadcasted_iota(jnp.int32, (BLOCK, n_ctx + 3 * BLOCK), 1) - n_ctx
    rel = kj - BLOCK - qi
    valid = (kj < 0) | ((jnp.abs(rel) <= BLOCK) & (kj >= lo) & (kj < hi))
    probs, dens = [], []
    for hh in range(GQA_GROUP):
        s_h = jnp.where(valid, s[hh * BLOCK:(hh + 1) * BLOCK], NEG_INF)
        mx = jnp.maximum(jnp.max(s_h, axis=-1, keepdims=True), sinks[hh])
        p = jnp.exp(s_h - mx)
        dens.append(jnp.sum(p, axis=-1, keepdims=True) + jnp.exp(sinks[hh] - mx))
        probs.append(p.astype(BF16))
    out = jnp.dot(jnp.concatenate(probs, axis=0), vv, preferred_element_type=F32)
    return out / jnp.concatenate(dens, axis=0)


def _mixer_kernel(sink_ref, x_ref, mod_ref, cos_ref, sin_ref,
                  kp_ref, km_ref, kn_ref, vp_ref, vm_ref, vn_ref, kc_ref, vc_ref,
                  win_ref, glg_ref, glb_ref, ws_ref, bs_ref, wa_ref, wb_ref, wo_ref,
                  l1g_ref, l1b_ref, o_ref, *, tile, n_tiles):
    i = pl.program_id(1)
    n_blk = tile // BLOCK
    x = x_ref[0]
    m = mod_ref[0]
    h = (_layer_norm(x) * (1.0 + m[1:2]) + m[0:1]).astype(BF16)

    def proj(off, width):
        return jnp.dot(h, win_ref[:, off:off + width], preferred_element_type=F32)

    q = proj(OFF_Q, Q_W)
    cos, sin = cos_ref[...], sin_ref[...]
    q = jnp.concatenate(
        [(_rope(q[:, c * LANES:(c + 1) * LANES], cos, sin) * ATTN_SCALE).astype(BF16)
         for c in range(Q_W // LANES)], axis=1)
    lo_first = jnp.where(i == 0, BLOCK, 0)
    hi_last = jnp.where(i == n_tiles - 1, 2 * BLOCK, 3 * BLOCK)
    head_out = [[None] * N_Q_HEADS for _ in range(n_blk)]
    for g in range(N_KV_HEADS):
        k_ext = jnp.concatenate([kp_ref[0, g], km_ref[0, g], kn_ref[0, g]], axis=0)
        v_ext = jnp.concatenate([vp_ref[0, g], vm_ref[0, g], vn_ref[0, g]], axis=0)
        kc, vc = kc_ref[0, g], vc_ref[0, g]
        sinks = [sink_ref[g * GQA_GROUP + hh] for hh in range(GQA_GROUP)]
        for j in range(n_blk):
            rows = slice(j * BLOCK, (j + 1) * BLOCK)
            q_blk = jnp.concatenate(
                [q[rows, (g * GQA_GROUP + hh) * HEAD_DIM:(g * GQA_GROUP + hh + 1) * HEAD_DIM]
                 for hh in range(GQA_GROUP)], axis=0)
            out = _attention_block(
                q_blk, kc, vc,
                k_ext[j * BLOCK:(j + 3) * BLOCK], v_ext[j * BLOCK:(j + 3) * BLOCK], sinks,
                lo_first if j == 0 else 0, hi_last if j == n_blk - 1 else 3 * BLOCK)
            for hh in range(GQA_GROUP):
                head_out[j][g * GQA_GROUP + hh] = out[hh * BLOCK:(hh + 1) * BLOCK]
    ya = jnp.concatenate([jnp.concatenate(head_out[j], axis=1) for j in range(n_blk)], axis=0)
    a_proj = jnp.dot(ya.astype(BF16), wa_ref[...], preferred_element_type=F32)

    u = _gelu_tanh(proj(OFF_U, GMLP_WIDTH))
    vn = (_layer_norm(_gelu_tanh(proj(OFF_VB, GMLP_WIDTH))) * glg_ref[...] + glb_ref[...]).astype(BF16)
    bias = bs_ref[...]
    gated = []
    for j in range(n_blk):
        rows = slice(j * BLOCK, (j + 1) * BLOCK)
        s_chunk = jnp.concatenate(
            [jnp.dot(ws_ref[gi], vn[rows, gi * GMLP_GROUP_DIM:(gi + 1) * GMLP_GROUP_DIM],
                     preferred_element_type=F32) for gi in range(N_GMLP_GROUPS)], axis=1)
        gated.append(u[rows] * (s_chunk + bias))
    yb = jnp.concatenate(gated, axis=0)
    b_proj = jnp.dot(yb.astype(BF16), wb_ref[...], preferred_element_type=F32)

    merged = (jax.nn.sigmoid(proj(OFF_GA, D_MODEL)) * a_proj
              + jax.nn.sigmoid(proj(OFF_GB, D_MODEL)) * b_proj)
    mix = jnp.dot(merged.astype(BF16), wo_ref[...], preferred_element_type=F32)
    o_ref[0] = _layer_norm(DEEPNORM_ALPHA * x + m[2:3] * mix) * l1g_ref[...] + l1b_ref[...]


def _mixer(x, mod, sink, tables, k, v, kc, vc, w_in_b, glg, glb, ws_b, bs_full, wa_b, wb_b, wo_b,
           l1g, l1b):
    bsz, seq, _ = x.shape
    tile = MIX_TILE
    n_tiles = seq // tile
    bpt = tile // BLOCK
    n_ctx = kc.shape[2]
    last_blk = seq // BLOCK - 1

    x_spec = pl.BlockSpec((1, tile, D_MODEL), lambda b, i: (b, i, 0))
    halo = (1, N_KV_HEADS, BLOCK, HEAD_DIM)
    prev_spec = pl.BlockSpec(halo, lambda b, i: (b, 0, jnp.maximum(i * bpt - 1, 0), 0))
    main_spec = pl.BlockSpec((1, N_KV_HEADS, tile, HEAD_DIM), lambda b, i: (b, 0, i, 0))
    next_spec = pl.BlockSpec(halo, lambda b, i: (b, 0, jnp.minimum((i + 1) * bpt, last_blk), 0))
    ctx_spec = pl.BlockSpec((1, N_KV_HEADS, n_ctx, HEAD_DIM), lambda b, i: (b, 0, 0, 0))
    table_spec = pl.BlockSpec((tile, LANES), lambda b, i: (i, 0))
    in_specs = [
        pl.BlockSpec(memory_space=pltpu.SMEM),
        x_spec,
        pl.BlockSpec((1, 6, D_MODEL), lambda b, i: (b, 0, 0)),
        table_spec, table_spec,
        prev_spec, main_spec, next_spec, prev_spec, main_spec, next_spec, ctx_spec, ctx_spec,
        _resident(w_in_b.shape), _resident(glg.shape), _resident(glb.shape), _resident(ws_b.shape),
        _resident(bs_full.shape), _resident(wa_b.shape), _resident(wb_b.shape), _resident(wo_b.shape),
        _resident(l1g.shape), _resident(l1b.shape),
    ]
    return pl.pallas_call(
        functools.partial(_mixer_kernel, tile=tile, n_tiles=n_tiles),
        grid=(bsz, n_tiles),
        in_specs=in_specs,
        out_specs=x_spec,
        out_shape=jax.ShapeDtypeStruct(x.shape, F32),
        compiler_params=pltpu.CompilerParams(
            dimension_semantics=("arbitrary", "arbitrary"), vmem_limit_bytes=VMEM_LIMIT),
        name="mixer",
    )(sink, x, mod, tables[0], tables[1], k, k, k, v, v, v, kc, vc,
      w_in_b, glg, glb, ws_b, bs_full, wa_b, wb_b, wo_b, l1g, l1b)


def _ffn_kernel(x_ref, mod_ref, w1_ref, w2_ref, l2g_ref, l2b_ref, o_ref, h_ref, acc_ref):
    x = x_ref[0]
    m = mod_ref[0]
    h_ref[...] = (_layer_norm(x) * (1.0 + m[4:5]) + m[3:4]).astype(BF16)
    acc_ref[...] = jnp.zeros_like(acc_ref)

    def chunk(c, carry):
        gu = jnp.dot(h_ref[...], w1_ref[c], preferred_element_type=F32)
        gate, up = gu[:, :FFN_CHUNK], gu[:, FFN_CHUNK:]
        act = (gate * jax.nn.sigmoid(gate) * up).astype(BF16)
        acc_ref[...] += jnp.dot(act, w2_ref[c], preferred_element_type=F32)
        return carry

    lax.fori_loop(0, FFN_HIDDEN // FFN_CHUNK, chunk, 0)
    o_ref[0] = _layer_norm(DEEPNORM_ALPHA * x + m[5:6] * acc_ref[...]) * l2g_ref[...] + l2b_ref[...]


def _ffn(x, mod, w1_b, w2_b, l2g, l2b):
    bsz, seq, _ = x.shape
    tile = FFN_TILE
    x_spec = pl.BlockSpec((1, tile, D_MODEL), lambda b, i: (b, i, 0))
    return pl.pallas_call(
        _ffn_kernel,
        grid=(bsz, seq // tile),
        in_specs=[x_spec, pl.BlockSpec((1, 6, D_MODEL), lambda b, i: (b, 0, 0)),
                  _resident(w1_b.shape), _resident(w2_b.shape),
                  _resident(l2g.shape), _resident(l2b.shape)],
        out_specs=x_spec,
        out_shape=jax.ShapeDtypeStruct(x.shape, F32),
        scratch_shapes=[pltpu.VMEM((tile, D_MODEL), BF16), pltpu.VMEM((tile, D_MODEL), F32)],
        compiler_params=pltpu.CompilerParams(
            dimension_semantics=("arbitrary", "arbitrary"), vmem_limit_bytes=VMEM_LIMIT),
        name="ffn",
    )(x, mod, w1_b, w2_b, l2g, l2b)


def _rope_tables(seq):
    inv = ROPE_BASE ** (-jnp.arange(ROPE_PAIRS, dtype=F32) / ROPE_PAIRS)
    t = jnp.arange(seq, dtype=jnp.int32)
    ar = (t // GRID_W).astype(F32)[:, None] * inv
    ac = (t % GRID_W).astype(F32)[:, None] * inv
    cos = jnp.concatenate([jnp.cos(ar), jnp.cos(ar), jnp.cos(ac), jnp.cos(ac)], axis=1)
    sin = jnp.concatenate([-jnp.sin(ar), jnp.sin(ar), -jnp.sin(ac), jnp.sin(ac)], axis=1)
    reps = LANES // HEAD_DIM
    return jnp.tile(cos, (1, reps)), jnp.tile(sin, (1, reps))


def kernel(x, c, ctx, c_ctx, w_ada, b_ada, w_in, attn_sink, gmlp_ln_g, gmlp_ln_b, w_spatial, b_spatial,
           w_branch_a, w_branch_b, w_out, ln1_g, ln1_b, w_ffn_in, w_ffn_out, ln2_g, ln2_b):
    bsz, seq, _ = x.shape
    n_ctx = ctx.shape[1]
    assert seq % MIX_TILE == 0 and seq % KV_TILE == 0 and seq % FFN_TILE == 0 and seq % GRID_W == 0
    assert w_ada.shape[0] == DEPTH

    pad = (-(bsz + 1)) % 8
    c_rows = jnp.concatenate([c, c_ctx[None], jnp.zeros((pad, D_MODEL), F32)], axis=0)
    mod = _ada_rows(c_rows, w_ada[0], b_ada[0]).reshape(c_rows.shape[0], 6, D_MODEL)

    w_in_b = w_in[0].astype(BF16)
    n_chunk = FFN_HIDDEN // FFN_CHUNK
    w1_b = (w_ffn_in[0].astype(BF16).reshape(D_MODEL, 2, n_chunk, FFN_CHUNK)
            .transpose(2, 0, 1, 3).reshape(n_chunk, D_MODEL, 2 * FFN_CHUNK))
    w2_b = w_ffn_out[0].astype(BF16).reshape(n_chunk, FFN_CHUNK, D_MODEL)
    bs_full = jnp.repeat(b_spatial[0].T, GMLP_GROUP_DIM, axis=1)
    row = lambda a: a[0].reshape(1, -1)

    tables = _rope_tables(seq)
    k, v = _kv_project(x, mod, lambda b: b, w_in_b, tables, KV_TILE)
    kc, vc = _kv_project(ctx, mod, lambda b: bsz, w_in_b, None, n_ctx)

    x_mid = _mixer(x, mod, attn_sink[0], tables, k, v, kc, vc, w_in_b,
                   row(gmlp_ln_g), row(gmlp_ln_b), w_spatial[0].astype(BF16), bs_full,
                   w_branch_a[0].astype(BF16), w_branch_b[0].astype(BF16), w_out[0].astype(BF16),
                   row(ln1_g), row(ln1_b))
    return _ffn(x_mid, mod, w1_b, w2_b, row(ln2_g), row(ln2_b))
```

```python
import functools

import jax
import jax.numpy as jnp
from jax import lax
from jax.experimental import pallas as pl
from jax.experimental.pallas import tpu as pltpu

F32 = jnp.float32
BF16 = jnp.bfloat16

D_MODEL = 1024
GRID_W = 64
HEAD_DIM = 64
N_Q_HEADS = 8
N_KV_HEADS = 2
GQA_GROUP = N_Q_HEADS // N_KV_HEADS
BLOCK = 128
ROPE_BASE = 10000.0
ROPE_PAIRS = HEAD_DIM // 4
N_GMLP_GROUPS = 8
GMLP_GROUP_DIM = 64
GMLP_WIDTH = N_GMLP_GROUPS * GMLP_GROUP_DIM
FFN_HIDDEN = 2816
Q_W = N_Q_HEADS * HEAD_DIM
KV_W = N_KV_HEADS * HEAD_DIM
OFF_Q, OFF_K, OFF_U = 0, Q_W, Q_W + 2 * KV_W
OFF_VB = OFF_U + GMLP_WIDTH
OFF_GA = OFF_VB + GMLP_WIDTH
OFF_GB = OFF_GA + D_MODEL
IN_W = OFF_GB + D_MODEL
LN_EPS = 1e-5
NEG_INF = -1e30
DEPTH = 1
DEEPNORM_ALPHA = (2 * DEPTH) ** 0.25
ATTN_SCALE = HEAD_DIM ** -0.5

LANES = 128
ADA_BLOCK_N = 1536
KV_TILE = 512
MIX_TILE = 512
FFN_TILE = 512
FFN_CHUNK = 256
VMEM_LIMIT = 56 * 1024 * 1024


def _layer_norm(x):
    mu = jnp.mean(x, axis=-1, keepdims=True)
    xc = x - mu
    var = jnp.mean(xc * xc, axis=-1, keepdims=True)
    return xc * lax.rsqrt(var + LN_EPS)


def _gelu_tanh(x):
    return 0.5 * x * (1.0 + jnp.tanh(0.7978845608028654 * (x + 0.044715 * (x * x * x))))


def _rope(t, cos, sin_signed):
    lane = lax.broadcasted_iota(jnp.int32, t.shape, 1)
    partner = jnp.where((lane & ROPE_PAIRS) == 0,
                        pltpu.roll(t, LANES - ROPE_PAIRS, 1), pltpu.roll(t, ROPE_PAIRS, 1))
    return t * cos + partner * sin_signed


def _resident(shape):
    zeros = (0,) * len(shape)
    return pl.BlockSpec(shape, lambda *_: zeros, pipeline_mode=pl.Buffered(1))


def _ada_kernel(c_ref, w_ref, b_ref, o_ref):
    c = c_ref[...]
    s = (c * jax.nn.sigmoid(c)).astype(BF16)
    o_ref[...] = jnp.dot(s, w_ref[...].astype(BF16), preferred_element_type=F32) + b_ref[...]


def _ada_rows(c_rows, w_ada, b_ada):
    n = w_ada.shape[1]
    return pl.pallas_call(
        _ada_kernel,
        grid=(n // ADA_BLOCK_N,),
        in_specs=[pl.BlockSpec(c_rows.shape, lambda j: (0, 0)),
                  pl.BlockSpec((D_MODEL, ADA_BLOCK_N), lambda j: (0, j)),
                  pl.BlockSpec((1, ADA_BLOCK_N), lambda j: (0, j))],
        out_specs=pl.BlockSpec((c_rows.shape[0], ADA_BLOCK_N), lambda j: (0, j)),
        out_shape=jax.ShapeDtypeStruct((c_rows.shape[0], n), F32),
        name="ada_rows",
    )(c_rows, w_ada, b_ada.reshape(1, n))


def _kv_kernel(*refs, rope):
    if rope:
        x_ref, mod_ref, w_ref, cos_ref, sin_ref, k_ref, v_ref = refs
    else:
        x_ref, mod_ref, w_ref, k_ref, v_ref = refs
    m = mod_ref[0]
    h = _layer_norm(x_ref[0]) * (1.0 + m[1:2]) + m[0:1]
    kv = jnp.dot(h.astype(BF16), w_ref[...], preferred_element_type=F32)
    k, v = kv[:, :KV_W], kv[:, KV_W:]
    if rope:
        k = _rope(k, cos_ref[...], sin_ref[...])
    for g in range(N_KV_HEADS):
        k_ref[0, g] = k[:, g * HEAD_DIM:(g + 1) * HEAD_DIM].astype(BF16)
        v_ref[0, g] = v[:, g * HEAD_DIM:(g + 1) * HEAD_DIM].astype(BF16)


def _kv_project(x, mod, mod_row, w_in_b, tables, tile):
    bsz, t, _ = x.shape
    rope = tables is not None
    in_specs = [pl.BlockSpec((1, tile, D_MODEL), lambda b, i: (b, i, 0)),
                pl.BlockSpec((1, 6, D_MODEL), lambda b, i: (mod_row(b), 0, 0)),
                pl.BlockSpec((D_MODEL, 2 * KV_W), lambda b, i: (0, OFF_K // (2 * KV_W)))]
    args = [x, mod, w_in_b]
    if rope:
        in_specs += [pl.BlockSpec((tile, LANES), lambda b, i: (i, 0))] * 2
        args += list(tables)
    out_spec = pl.BlockSpec((1, N_KV_HEADS, tile, HEAD_DIM), lambda b, i: (b, 0, i, 0))
    out_shape = jax.ShapeDtypeStruct((bsz, N_KV_HEADS, t, HEAD_DIM), BF16)
    return pl.pallas_call(
        functools.partial(_kv_kernel, rope=rope),
        grid=(bsz, t // tile),
        in_specs=in_specs,
        out_specs=[out_spec, out_spec],
        out_shape=[out_shape, out_shape],
        name="kv_latent" if rope else "kv_context",
    )(*args)


def _attention_block(q_blk, kc, vc, k_loc, v_loc, sinks, lo, hi):
    n_ctx = kc.shape[0]
    kk = jnp.concatenate([kc, k_loc], axis=0)
    vv = jnp.concatenate([vc, v_loc], axis=0)
    s = lax.dot_general(q_blk, kk, (((1,), (1,)), ((), ())), preferred_element_type=F32)
    qi = lax.broadcasted_iota(jnp.int32, (BLOCK, n_ctx + 3 * BLOCK), 0)
    kj = lax.broadcasted_iota(jnp.int32, (BLOCK, n_ctx + 3 * BLOCK), 1) - n_ctx
    rel = kj - BLOCK - qi
    valid = (kj < 0) | ((jnp.abs(rel) <= BLOCK) & (kj >= lo) & (kj < hi))
    probs, dens = [], []
    for hh in range(GQA_GROUP):
        s_h = jnp.where(valid, s[hh * BLOCK:(hh + 1) * BLOCK], NEG_INF)
        mx = jnp.maximum(jnp.max(s_h, axis=-1, keepdims=True), sinks[hh])
        p = jnp.exp(s_h - mx)
        dens.append(jnp.sum(p, axis=-1, keepdims=True) + jnp.exp(sinks[hh] - mx))
        probs.append(p.astype(BF16))
    out = jnp.dot(jnp.concatenate(probs, axis=0), vv, preferred_element_type=F32)
    return out / jnp.concatenate(dens, axis=0)


def _mixer_kernel(sink_ref, x_ref, mod_ref, cos_ref, sin_ref,
                  kp_ref, km_ref, kn_ref, vp_ref, vm_ref, vn_ref, kc_ref, vc_ref,
                  win_ref, glg_ref, glb_ref, ws_ref, bs_ref, wa_ref, wb_ref, wo_ref,
                  l1g_ref, l1b_ref, o_ref, *, tile, n_tiles):
    i = pl.program_id(1)
    n_blk = tile // BLOCK
    x = x_ref[0]
    m = mod_ref[0]
    h = (_layer_norm(x) * (1.0 + m[1:2]) + m[0:1]).astype(BF16)

    def proj(off, width):
        return jnp.dot(h, win_ref[:, off:off + width], preferred_element_type=F32)

    q = proj(OFF_Q, Q_W)
    cos, sin = cos_ref[...], sin_ref[...]
    q = jnp.concatenate(
        [(_rope(q[:, c * LANES:(c + 1) * LANES], cos, sin) * ATTN_SCALE).astype(BF16)
         for c in range(Q_W // LANES)], axis=1)
    lo_first = jnp.where(i == 0, BLOCK, 0)
    hi_last = jnp.where(i == n_tiles - 1, 2 * BLOCK, 3 * BLOCK)
    head_out = [[None] * N_Q_HEADS for _ in range(n_blk)]
    for g in range(N_KV_HEADS):
        k_ext = jnp.concatenate([kp_ref[0, g], km_ref[0, g], kn_ref[0, g]], axis=0)
        v_ext = jnp.concatenate([vp_ref[0, g], vm_ref[0, g], vn_ref[0, g]], axis=0)
        kc, vc = kc_ref[0, g], vc_ref[0, g]
        sinks = [sink_ref[g * GQA_GROUP + hh] for hh in range(GQA_GROUP)]
        for j in range(n_blk):
            rows = slice(j * BLOCK, (j + 1) * BLOCK)
            q_blk = jnp.concatenate(
                [q[rows, (g * GQA_GROUP + hh) * HEAD_DIM:(g * GQA_GROUP + hh + 1) * HEAD_DIM]
                 for hh in range(GQA_GROUP)], axis=0)
            out = _attention_block(
                q_blk, kc, vc,
                k_ext[j * BLOCK:(j + 3) * BLOCK], v_ext[j * BLOCK:(j + 3) * BLOCK], sinks,
                lo_first if j == 0 else 0, hi_last if j == n_blk - 1 else 3 * BLOCK)
            for hh in range(GQA_GROUP):
                head_out[j][g * GQA_GROUP + hh] = out[hh * BLOCK:(hh + 1) * BLOCK]
    ya = jnp.concatenate([jnp.concatenate(head_out[j], axis=1) for j in range(n_blk)], axis=0)
    a_proj = jnp.dot(ya.astype(BF16), wa_ref[...], preferred_element_type=F32)

    u = _gelu_tanh(proj(OFF_U, GMLP_WIDTH))
    vn = (_layer_norm(_gelu_tanh(proj(OFF_VB, GMLP_WIDTH))) * glg_ref[...] + glb_ref[...]).astype(BF16)
    bias = bs_ref[...]
    gated = []
    for j in range(n_blk):
        rows = slice(j * BLOCK, (j + 1) * BLOCK)
        s_chunk = jnp.concatenate(
            [jnp.dot(ws_ref[gi], vn[rows, gi * GMLP_GROUP_DIM:(gi + 1) * GMLP_GROUP_DIM],
                     preferred_element_type=F32) for gi in range(N_GMLP_GROUPS)], axis=1)
        gated.append(u[rows] * (s_chunk + bias))
    yb = jnp.concatenate(gated, axis=0)
    b_proj = jnp.dot(yb.astype(BF16), wb_ref[...], preferred_element_type=F32)

    merged = (jax.nn.sigmoid(proj(OFF_GA, D_MODEL)) * a_proj
              + jax.nn.sigmoid(proj(OFF_GB, D_MODEL)) * b_proj)
    mix = jnp.dot(merged.astype(BF16), wo_ref[...], preferred_element_type=F32)
    o_ref[0] = _layer_norm(DEEPNORM_ALPHA * x + m[2:3] * mix) * l1g_ref[...] + l1b_ref[...]


def _mixer(x, mod, sink, tables, k, v, kc, vc, w_in_b, glg, glb, ws_b, bs_full, wa_b, wb_b, wo_b,
           l1g, l1b):
    bsz, seq, _ = x.shape
    tile = MIX_TILE
    n_tiles = seq // tile
    bpt = tile // BLOCK
    n_ctx = kc.shape[2]
    last_blk = seq // BLOCK - 1

    x_spec = pl.BlockSpec((1, tile, D_MODEL), lambda b, i: (b, i, 0))
    halo = (1, N_KV_HEADS, BLOCK, HEAD_DIM)
    prev_spec = pl.BlockSpec(halo, lambda b, i: (b, 0, jnp.maximum(i * bpt - 1, 0), 0))
    main_spec = pl.BlockSpec((1, N_KV_HEADS, tile, HEAD_DIM), lambda b, i: (b, 0, i, 0))
    next_spec = pl.BlockSpec(halo, lambda b, i: (b, 0, jnp.minimum((i + 1) * bpt, last_blk), 0))
    ctx_spec = pl.BlockSpec((1, N_KV_HEADS, n_ctx, HEAD_DIM), lambda b, i: (b, 0, 0, 0))
    table_spec = pl.BlockSpec((tile, LANES), lambda b, i: (i, 0))
    in_specs = [
        pl.BlockSpec(memory_space=pltpu.SMEM),
        x_spec,
        pl.BlockSpec((1, 6, D_MODEL), lambda b, i: (b, 0, 0)),
        table_spec, table_spec,
        prev_spec, main_spec, next_spec, prev_spec, main_spec, next_spec, ctx_spec, ctx_spec,
        _resident(w_in_b.shape), _resident(glg.shape), _resident(glb.shape), _resident(ws_b.shape),
        _resident(bs_full.shape), _resident(wa_b.shape), _resident(wb_b.shape), _resident(wo_b.shape),
        _resident(l1g.shape), _resident(l1b.shape),
    ]
    return pl.pallas_call(
        functools.partial(_mixer_kernel, tile=tile, n_tiles=n_tiles),
        grid=(bsz, n_tiles),
        in_specs=in_specs,
        out_specs=x_spec,
        out_shape=jax.ShapeDtypeStruct(x.shape, F32),
        compiler_params=pltpu.CompilerParams(
            dimension_semantics=("arbitrary", "arbitrary"), vmem_limit_bytes=VMEM_LIMIT),
        name="mixer",
    )(sink, x, mod, tables[0], tables[1], k, k, k, v, v, v, kc, vc,
      w_in_b, glg, glb, ws_b, bs_full, wa_b, wb_b, wo_b, l1g, l1b)


def _ffn_kernel(x_ref, mod_ref, w1_ref, w2_ref, l2g_ref, l2b_ref, o_ref):
    x = x_ref[0]
    m = mod_ref[0]
    h = (_layer_norm(x) * (1.0 + m[4:5]) + m[3:4]).astype(BF16)
    acc = None
    for c in range(FFN_HIDDEN // FFN_CHUNK):
        lo = c * FFN_CHUNK
        gate = jnp.dot(h, w1_ref[:, lo:lo + FFN_CHUNK], preferred_element_type=F32)
        up = jnp.dot(h, w1_ref[:, FFN_HIDDEN + lo:FFN_HIDDEN + lo + FFN_CHUNK], preferred_element_type=F32)
        act = (gate * jax.nn.sigmoid(gate) * up).astype(BF16)
        part = jnp.dot(act, w2_ref[lo:lo + FFN_CHUNK, :], preferred_element_type=F32)
        acc = part if acc is None else acc + part
    o_ref[0] = _layer_norm(DEEPNORM_ALPHA * x + m[5:6] * acc) * l2g_ref[...] + l2b_ref[...]


def _ffn(x, mod, w1_b, w2_b, l2g, l2b):
    bsz, seq, _ = x.shape
    tile = FFN_TILE
    x_spec = pl.BlockSpec((1, tile, D_MODEL), lambda b, i: (b, i, 0))
    return pl.pallas_call(
        _ffn_kernel,
        grid=(bsz, seq // tile),
        in_specs=[x_spec, pl.BlockSpec((1, 6, D_MODEL), lambda b, i: (b, 0, 0)),
                  _resident(w1_b.shape), _resident(w2_b.shape),
                  _resident(l2g.shape), _resident(l2b.shape)],
        out_specs=x_spec,
        out_shape=jax.ShapeDtypeStruct(x.shape, F32),
        compiler_params=pltpu.CompilerParams(
            dimension_semantics=("arbitrary", "arbitrary"), vmem_limit_bytes=VMEM_LIMIT),
        name="ffn",
    )(x, mod, w1_b, w2_b, l2g, l2b)


def _rope_tables(seq):
    inv = ROPE_BASE ** (-jnp.arange(ROPE_PAIRS, dtype=F32) / ROPE_PAIRS)
    t = jnp.arange(seq, dtype=jnp.int32)
    ar = (t // GRID_W).astype(F32)[:, None] * inv
    ac = (t % GRID_W).astype(F32)[:, None] * inv
    cos = jnp.concatenate([jnp.cos(ar), jnp.cos(ar), jnp.cos(ac), jnp.cos(ac)], axis=1)
    sin = jnp.concatenate([-jnp.sin(ar), jnp.sin(ar), -jnp.sin(ac), jnp.sin(ac)], axis=1)
    reps = LANES // HEAD_DIM
    return jnp.tile(cos, (1, reps)), jnp.tile(sin, (1, reps))


def kernel(x, c, ctx, c_ctx, w_ada, b_ada, w_in, attn_sink, gmlp_ln_g, gmlp_ln_b, w_spatial, b_spatial,
           w_branch_a, w_branch_b, w_out, ln1_g, ln1_b, w_ffn_in, w_ffn_out, ln2_g, ln2_b):
    bsz, seq, _ = x.shape
    n_ctx = ctx.shape[1]
    assert seq % MIX_TILE == 0 and seq % KV_TILE == 0 and seq % FFN_TILE == 0 and seq % GRID_W == 0
    assert w_ada.shape[0] == DEPTH

    pad = (-(bsz + 1)) % 8
    c_rows = jnp.concatenate([c, c_ctx[None], jnp.zeros((pad, D_MODEL), F32)], axis=0)
    mod = _ada_rows(c_rows, w_ada[0], b_ada[0]).reshape(c_rows.shape[0], 6, D_MODEL)

    w_in_b = w_in[0].astype(BF16)
    w1_b = w_ffn_in[0].astype(BF16)
    w2_b = w_ffn_out[0].astype(BF16)
    bs_full = jnp.repeat(b_spatial[0].T, GMLP_GROUP_DIM, axis=1)
    row = lambda a: a[0].reshape(1, -1)

    tables = _rope_tables(seq)
    k, v = _kv_project(x, mod, lambda b: b, w_in_b, tables, KV_TILE)
    kc, vc = _kv_project(ctx, mod, lambda b: bsz, w_in_b, None, n_ctx)

    x_mid = _mixer(x, mod, attn_sink[0], tables, k, v, kc, vc, w_in_b,
                   row(gmlp_ln_g), row(gmlp_ln_b), w_spatial[0].astype(BF16), bs_full,
                   w_branch_a[0].astype(BF16), w_branch_b[0].astype(BF16), w_out[0].astype(BF16),
                   row(ln1_g), row(ln1_b))
    return _ffn(x_mid, mod, w1_b, w2_b, row(ln2_g), row(ln2_b))
```

```python
import functools
import math

import jax
import jax.numpy as jnp
from jax import lax
from jax.experimental import pallas as pl
from jax.experimental.pallas import tpu as pltpu

F32 = jnp.float32
BF16 = jnp.bfloat16

D_MODEL = 1024
GRID_W = 64
HEAD_DIM = 64
N_Q_HEADS = 8
N_KV_HEADS = 2
GQA_GROUP = N_Q_HEADS // N_KV_HEADS
BLOCK = 128
ROPE_BASE = 10000.0
ROPE_PAIRS = HEAD_DIM // 4
N_GMLP_GROUPS = 8
GMLP_GROUP_DIM = 64
GMLP_WIDTH = N_GMLP_GROUPS * GMLP_GROUP_DIM
FFN_HIDDEN = 2816
Q_W = N_Q_HEADS * HEAD_DIM
KV_W = N_KV_HEADS * HEAD_DIM
OFF_Q, OFF_K, OFF_U = 0, Q_W, Q_W + 2 * KV_W
OFF_VB = OFF_U + GMLP_WIDTH
OFF_GA = OFF_VB + GMLP_WIDTH
OFF_GB = OFF_GA + D_MODEL
IN_W = OFF_GB + D_MODEL
LN_EPS = 1e-5
NEG_INF = -1e30
DEPTH = 1
DEEPNORM_ALPHA = (2 * DEPTH) ** 0.25
LOG2E = math.log2(math.e)
Q_SCALE = HEAD_DIM ** -0.5 * LOG2E

LANES = 128
ADA_BLOCK_N = 1536
KV_TILE = 512
MIX_TILE = 512
FFN_TILE = 512
FFN_CHUNK = 256
VMEM_LIMIT = 56 * 1024 * 1024


def _layer_norm(x):
    mu = jnp.mean(x, axis=-1, keepdims=True)
    xc = x - mu
    var = jnp.mean(xc * xc, axis=-1, keepdims=True)
    return xc * lax.rsqrt(var + LN_EPS)


def _gelu_tanh(x):
    return 0.5 * x * (1.0 + jnp.tanh(0.7978845608028654 * (x + 0.044715 * (x * x * x))))


def _rope(t, cos, sin_signed):
    lane = lax.broadcasted_iota(jnp.int32, t.shape, 1)
    partner = jnp.where((lane & ROPE_PAIRS) == 0,
                        pltpu.roll(t, LANES - ROPE_PAIRS, 1), pltpu.roll(t, ROPE_PAIRS, 1))
    return t * cos + partner * sin_signed


def _resident(shape):
    zeros = (0,) * len(shape)
    return pl.BlockSpec(shape, lambda *_: zeros, pipeline_mode=pl.Buffered(1))


def _ada_kernel(c_ref, w_ref, b_ref, o_ref):
    c = c_ref[...]
    s = (c * jax.nn.sigmoid(c)).astype(BF16)
    o_ref[...] = jnp.dot(s, w_ref[...].astype(BF16), preferred_element_type=F32) + b_ref[...]


def _ada_rows(c_rows, w_ada, b_ada):
    n = w_ada.shape[1]
    return pl.pallas_call(
        _ada_kernel,
        grid=(n // ADA_BLOCK_N,),
        in_specs=[pl.BlockSpec(c_rows.shape, lambda j: (0, 0)),
                  pl.BlockSpec((D_MODEL, ADA_BLOCK_N), lambda j: (0, j)),
                  pl.BlockSpec((1, ADA_BLOCK_N), lambda j: (0, j))],
        out_specs=pl.BlockSpec((c_rows.shape[0], ADA_BLOCK_N), lambda j: (0, j)),
        out_shape=jax.ShapeDtypeStruct((c_rows.shape[0], n), F32),
        name="ada_rows",
    )(c_rows, w_ada, b_ada.reshape(1, n))


def _kv_kernel(*refs, rope):
    if rope:
        x_ref, mod_ref, w_ref, cos_ref, sin_ref, k_ref, vt_ref = refs
    else:
        x_ref, mod_ref, w_ref, k_ref, vt_ref = refs
    m = mod_ref[0]
    h = _layer_norm(x_ref[0]) * (1.0 + m[1:2]) + m[0:1]
    kv = jnp.dot(h.astype(BF16), w_ref[...], preferred_element_type=F32)
    k, vt = kv[:, :KV_W], kv[:, KV_W:].T
    if rope:
        k = _rope(k, cos_ref[...], sin_ref[...])
    for g in range(N_KV_HEADS):
        k_ref[0, g] = k[:, g * HEAD_DIM:(g + 1) * HEAD_DIM].astype(BF16)
        vt_ref[0, g] = vt[g * HEAD_DIM:(g + 1) * HEAD_DIM, :].astype(BF16)


def _kv_project(x, mod, mod_row, w_in_b, tables, tile):
    bsz, t, _ = x.shape
    rope = tables is not None
    in_specs = [pl.BlockSpec((1, tile, D_MODEL), lambda b, i: (b, i, 0)),
                pl.BlockSpec((1, 6, D_MODEL), lambda b, i: (mod_row(b), 0, 0)),
                pl.BlockSpec((D_MODEL, 2 * KV_W), lambda b, i: (0, OFF_K // (2 * KV_W)))]
    args = [x, mod, w_in_b]
    if rope:
        in_specs += [pl.BlockSpec((tile, LANES), lambda b, i: (i, 0))] * 2
        args += list(tables)
    return pl.pallas_call(
        functools.partial(_kv_kernel, rope=rope),
        grid=(bsz, t // tile),
        in_specs=in_specs,
        out_specs=[pl.BlockSpec((1, N_KV_HEADS, tile, HEAD_DIM), lambda b, i: (b, 0, i, 0)),
                   pl.BlockSpec((1, N_KV_HEADS, HEAD_DIM, tile), lambda b, i: (b, 0, 0, i))],
        out_shape=[jax.ShapeDtypeStruct((bsz, N_KV_HEADS, t, HEAD_DIM), BF16),
                   jax.ShapeDtypeStruct((bsz, N_KV_HEADS, HEAD_DIM, t), BF16)],
        name="kv_latent" if rope else "kv_context",
    )(*args)


def _attention_unit(q_blk, kk, vvt, bias_prev, bias_next, sink_row):
    n_ctx = kk.shape[0] - 3 * BLOCK
    st = lax.dot_general(kk, q_blk, (((1,), (1,)), ((), ())), preferred_element_type=F32)
    slabs = [st[:n_ctx + BLOCK],
             st[n_ctx + BLOCK:n_ctx + 2 * BLOCK],
             st[n_ctx + 2 * BLOCK:]]
    slabs[0] = jnp.concatenate([slabs[0][:n_ctx], slabs[0][n_ctx:] + bias_prev], axis=0)
    slabs[2] = slabs[2] + bias_next
    mx = sink_row
    for sl in slabs:
        mx = jnp.maximum(mx, jnp.max(sl, axis=0, keepdims=True))
    den = jnp.exp2(sink_row - mx)
    probs = []
    for sl in slabs:
        p = jnp.exp2(sl - mx)
        den = den + jnp.sum(p, axis=0, keepdims=True)
        probs.append(p.astype(BF16))
    out_t = jnp.dot(vvt, jnp.concatenate(probs, axis=0), preferred_element_type=F32)
    return out_t / den


def _mixer_kernel(x_ref, mod_ref, cos_ref, sin_ref, sink_ref,
                  kp_ref, km_ref, kn_ref, vp_ref, vm_ref, vn_ref, kc_ref, vc_ref,
                  win_ref, glg_ref, glb_ref, ws_ref, bs_ref, wa_ref, wb_ref, wo_ref,
                  l1g_ref, l1b_ref, o_ref, *, tile, n_tiles):
    i = pl.program_id(1)
    n_blk = tile // BLOCK
    x = x_ref[0]
    m = mod_ref[0]
    h = (_layer_norm(x) * (1.0 + m[1:2]) + m[0:1]).astype(BF16)

    def proj(off, width):
        return jnp.dot(h, win_ref[:, off:off + width], preferred_element_type=F32)

    q = proj(OFF_Q, Q_W)
    cos, sin = cos_ref[...], sin_ref[...]
    q = jnp.concatenate(
        [(_rope(q[:, c * LANES:(c + 1) * LANES], cos, sin) * Q_SCALE).astype(BF16)
         for c in range(Q_W // LANES)], axis=1)

    kj = lax.broadcasted_iota(jnp.int32, (BLOCK, GQA_GROUP * BLOCK), 0)
    qi = lax.broadcasted_iota(jnp.int32, (BLOCK, GQA_GROUP * BLOCK), 1) & (BLOCK - 1)
    band_prev = jnp.where(kj >= qi, 0.0, NEG_INF)
    band_next = jnp.where(kj <= qi, 0.0, NEG_INF)
    first_prev = jnp.where(i == 0, NEG_INF, band_prev)
    last_next = jnp.where(i == n_tiles - 1, NEG_INF, band_next)

    head_out = [[None] * n_blk for _ in range(N_Q_HEADS)]
    for g in range(N_KV_HEADS):
        k_ext = jnp.concatenate([kp_ref[0, g], km_ref[0, g], kn_ref[0, g]], axis=0)
        vt_ext = jnp.concatenate([vp_ref[0, g], vm_ref[0, g], vn_ref[0, g]], axis=1)
        kc, vct = kc_ref[0, g], vc_ref[0, g]
        sink_row = sink_ref[g:g + 1, :] * LOG2E
        for j in range(n_blk):
            rows = slice(j * BLOCK, (j + 1) * BLOCK)
            q_blk = jnp.concatenate(
                [q[rows, (g * GQA_GROUP + hh) * HEAD_DIM:(g * GQA_GROUP + hh + 1) * HEAD_DIM]
                 for hh in range(GQA_GROUP)], axis=0)
            out_t = _attention_unit(
                q_blk,
                jnp.concatenate([kc, k_ext[j * BLOCK:(j + 3) * BLOCK]], axis=0),
                jnp.concatenate([vct, vt_ext[:, j * BLOCK:(j + 3) * BLOCK]], axis=1),
                first_prev if j == 0 else band_prev,
                last_next if j == n_blk - 1 else band_next,
                sink_row)
            for hh in range(GQA_GROUP):
                head_out[g * GQA_GROUP + hh][j] = out_t[:, hh * BLOCK:(hh + 1) * BLOCK]
    ya_t = jnp.concatenate([jnp.concatenate(head_out[hd], axis=1) for hd in range(N_Q_HEADS)], axis=0)
    a_proj = jnp.dot(ya_t.T.astype(BF16), wa_ref[...], preferred_element_type=F32)

    u = _gelu_tanh(proj(OFF_U, GMLP_WIDTH))
    vn = (_layer_norm(_gelu_tanh(proj(OFF_VB, GMLP_WIDTH))) * glg_ref[...] + glb_ref[...]).astype(BF16)
    bias = bs_ref[...]
    gated = []
    for j in range(n_blk):
        rows = slice(j * BLOCK, (j + 1) * BLOCK)
        s_chunk = jnp.concatenate(
            [jnp.dot(ws_ref[gi], vn[rows, gi * GMLP_GROUP_DIM:(gi + 1) * GMLP_GROUP_DIM],
                     preferred_element_type=F32) for gi in range(N_GMLP_GROUPS)], axis=1)
        gated.append(u[rows] * (s_chunk + bias))
    yb = jnp.concatenate(gated, axis=0)
    b_proj = jnp.dot(yb.astype(BF16), wb_ref[...], preferred_element_type=F32)

    merged = (jax.nn.sigmoid(proj(OFF_GA, D_MODEL)) * a_proj
              + jax.nn.sigmoid(proj(OFF_GB, D_MODEL)) * b_proj)
    mix = jnp.dot(merged.astype(BF16), wo_ref[...], preferred_element_type=F32)
    o_ref[0] = _layer_norm(DEEPNORM_ALPHA * x + m[2:3] * mix) * l1g_ref[...] + l1b_ref[...]


def _mixer(x, mod, sink_rows, tables, k, vt, kc, vct, w_in_b, glg, glb, ws_b, bs_full, wa_b, wb_b, wo_b,
           l1g, l1b):
    bsz, seq, _ = x.shape
    tile = MIX_TILE
    n_tiles = seq // tile
    bpt = tile // BLOCK
    n_ctx = kc.shape[2]
    last_blk = seq // BLOCK - 1

    def prev_blk(i):
        return jnp.maximum(i * bpt - 1, 0)

    def next_blk(i):
        return jnp.minimum((i + 1) * bpt, last_blk)

    x_spec = pl.BlockSpec((1, tile, D_MODEL), lambda b, i: (b, i, 0))
    table_spec = pl.BlockSpec((tile, LANES), lambda b, i: (i, 0))
    k_halo = (1, N_KV_HEADS, BLOCK, HEAD_DIM)
    vt_halo = (1, N_KV_HEADS, HEAD_DIM, BLOCK)
    in_specs = [
        x_spec,
        pl.BlockSpec((1, 6, D_MODEL), lambda b, i: (b, 0, 0)),
        table_spec, table_spec,
        _resident(sink_rows.shape),
        pl.BlockSpec(k_halo, lambda b, i: (b, 0, prev_blk(i), 0)),
        pl.BlockSpec((1, N_KV_HEADS, tile, HEAD_DIM), lambda b, i: (b, 0, i, 0)),
        pl.BlockSpec(k_halo, lambda b, i: (b, 0, next_blk(i), 0)),
        pl.BlockSpec(vt_halo, lambda b, i: (b, 0, 0, prev_blk(i))),
        pl.BlockSpec((1, N_KV_HEADS, HEAD_DIM, tile), lambda b, i: (b, 0, 0, i)),
        pl.BlockSpec(vt_halo, lambda b, i: (b, 0, 0, next_blk(i))),
        pl.BlockSpec((1, N_KV_HEADS, n_ctx, HEAD_DIM), lambda b, i: (b, 0, 0, 0)),
        pl.BlockSpec((1, N_KV_HEADS, HEAD_DIM, n_ctx), lambda b, i: (b, 0, 0, 0)),
        _resident(w_in_b.shape), _resident(glg.shape), _resident(glb.shape), _resident(ws_b.shape),
        _resident(bs_full.shape), _resident(wa_b.shape), _resident(wb_b.shape), _resident(wo_b.shape),
        _resident(l1g.shape), _resident(l1b.shape),
    ]
    return pl.pallas_call(
        functools.partial(_mixer_kernel, tile=tile, n_tiles=n_tiles),
        grid=(bsz, n_tiles),
        in_specs=in_specs,
        out_specs=x_spec,
        out_shape=jax.ShapeDtypeStruct(x.shape, F32),
        compiler_params=pltpu.CompilerParams(
            dimension_semantics=("arbitrary", "arbitrary"), vmem_limit_bytes=VMEM_LIMIT),
        name="mixer",
    )(x, mod, tables[0], tables[1], sink_rows, k, k, k, vt, vt, vt, kc, vct,
      w_in_b, glg, glb, ws_b, bs_full, wa_b, wb_b, wo_b, l1g, l1b)


def _ffn_kernel(x_ref, mod_ref, w1_ref, w2_ref, l2g_ref, l2b_ref, o_ref):
    x = x_ref[0]
    m = mod_ref[0]
    h = (_layer_norm(x) * (1.0 + m[4:5]) + m[3:4]).astype(BF16)
    acc = None
    for c in range(FFN_HIDDEN // FFN_CHUNK):
        lo = c * FFN_CHUNK
        gate = jnp.dot(h, w1_ref[:, lo:lo + FFN_CHUNK], preferred_element_type=F32)
        up = jnp.dot(h, w1_ref[:, FFN_HIDDEN + lo:FFN_HIDDEN + lo + FFN_CHUNK], preferred_element_type=F32)
        act = (gate * jax.nn.sigmoid(gate) * up).astype(BF16)
        part = jnp.dot(act, w2_ref[lo:lo + FFN_CHUNK, :], preferred_element_type=F32)
        acc = part if acc is None else acc + part
    o_ref[0] = _layer_norm(DEEPNORM_ALPHA * x + m[5:6] * acc) * l2g_ref[...] + l2b_ref[...]


def _ffn(x, mod, w1_b, w2_b, l2g, l2b):
    bsz, seq, _ = x.shape
    tile = FFN_TILE
    x_spec = pl.BlockSpec((1, tile, D_MODEL), lambda b, i: (b, i, 0))
    return pl.pallas_call(
        _ffn_kernel,
        grid=(bsz, seq // tile),
        in_specs=[x_spec, pl.BlockSpec((1, 6, D_MODEL), lambda b, i: (b, 0, 0)),
                  _resident(w1_b.shape), _resident(w2_b.shape),
                  _resident(l2g.shape), _resident(l2b.shape)],
        out_specs=x_spec,
        out_shape=jax.ShapeDtypeStruct(x.shape, F32),
        compiler_params=pltpu.CompilerParams(
            dimension_semantics=("arbitrary", "arbitrary"), vmem_limit_bytes=VMEM_LIMIT),
        name="ffn",
    )(x, mod, w1_b, w2_b, l2g, l2b)


def _rope_tables(seq):
    inv = ROPE_BASE ** (-jnp.arange(ROPE_PAIRS, dtype=F32) / ROPE_PAIRS)
    t = jnp.arange(seq, dtype=jnp.int32)
    ar = (t // GRID_W).astype(F32)[:, None] * inv
    ac = (t % GRID_W).astype(F32)[:, None] * inv
    cos = jnp.concatenate([jnp.cos(ar), jnp.cos(ar), jnp.cos(ac), jnp.cos(ac)], axis=1)
    sin = jnp.concatenate([-jnp.sin(ar), jnp.sin(ar), -jnp.sin(ac), jnp.sin(ac)], axis=1)
    reps = LANES // HEAD_DIM
    return jnp.tile(cos, (1, reps)), jnp.tile(sin, (1, reps))


def kernel(x, c, ctx, c_ctx, w_ada, b_ada, w_in, attn_sink, gmlp_ln_g, gmlp_ln_b, w_spatial, b_spatial,
           w_branch_a, w_branch_b, w_out, ln1_g, ln1_b, w_ffn_in, w_ffn_out, ln2_g, ln2_b):
    bsz, seq, _ = x.shape
    n_ctx = ctx.shape[1]
    assert seq % MIX_TILE == 0 and seq % KV_TILE == 0 and seq % FFN_TILE == 0 and seq % GRID_W == 0
    assert w_ada.shape[0] == DEPTH

    pad = (-(bsz + 1)) % 8
    c_rows = jnp.concatenate([c, c_ctx[None], jnp.zeros((pad, D_MODEL), F32)], axis=0)
    mod = _ada_rows(c_rows, w_ada[0], b_ada[0]).reshape(c_rows.shape[0], 6, D_MODEL)

    w_in_b = w_in[0].astype(BF16)
    bs_full = jnp.repeat(b_spatial[0].T, GMLP_GROUP_DIM, axis=1)
    sink_rows = jnp.repeat(attn_sink[0], BLOCK).reshape(N_KV_HEADS, GQA_GROUP * BLOCK)
    row = lambda a: a[0].reshape(1, -1)

    tables = _rope_tables(seq)
    k, vt = _kv_project(x, mod, lambda b: b, w_in_b, tables, KV_TILE)
    kc, vct = _kv_project(ctx, mod, lambda b: bsz, w_in_b, None, n_ctx)

    x_mid = _mixer(x, mod, sink_rows, tables, k, vt, kc, vct, w_in_b,
                   row(gmlp_ln_g), row(gmlp_ln_b), w_spatial[0].astype(BF16), bs_full,
                   w_branch_a[0].astype(BF16), w_branch_b[0].astype(BF16), w_out[0].astype(BF16),
                   row(ln1_g), row(ln1_b))
    return _ffn(x_mid, mod, w_ffn_in[0].astype(BF16), w_ffn_out[0].astype(BF16), row(ln2_g), row(ln2_b))
```

```python
import functools
import math

import jax
import jax.numpy as jnp
from jax import lax
from jax.experimental import pallas as pl
from jax.experimental.pallas import tpu as pltpu

F32 = jnp.float32
BF16 = jnp.bfloat16

D_MODEL = 1024
GRID_W = 64
HEAD_DIM = 64
N_Q_HEADS = 8
N_KV_HEADS = 2
GQA_GROUP = N_Q_HEADS // N_KV_HEADS
BLOCK = 128
ROPE_BASE = 10000.0
ROPE_PAIRS = HEAD_DIM // 4
N_GMLP_GROUPS = 8
GMLP_GROUP_DIM = 64
GMLP_WIDTH = N_GMLP_GROUPS * GMLP_GROUP_DIM
FFN_HIDDEN = 2816
Q_W = N_Q_HEADS * HEAD_DIM
KV_W = N_KV_HEADS * HEAD_DIM
OFF_Q, OFF_K, OFF_U = 0, Q_W, Q_W + 2 * KV_W
OFF_VB = OFF_U + GMLP_WIDTH
OFF_GA = OFF_VB + GMLP_WIDTH
OFF_GB = OFF_GA + D_MODEL
IN_W = OFF_GB + D_MODEL
LN_EPS = 1e-5
NEG_INF = -1e30
DEPTH = 1
DEEPNORM_ALPHA = (2 * DEPTH) ** 0.25
LOG2E = math.log2(math.e)
Q_SCALE = HEAD_DIM ** -0.5 * LOG2E

LANES = 128
ADA_BLOCK_N = 1536
KV_TILE = 512
MIX_TILE = 512
FFN_TILE = 512
FFN_CHUNK = 256
ROW_PARTS = 2
HEAD_PIECES = 2
VMEM_LIMIT = 56 * 1024 * 1024


def _layer_norm(x):
    mu = jnp.mean(x, axis=-1, keepdims=True)
    xc = x - mu
    var = jnp.mean(xc * xc, axis=-1, keepdims=True)
    return xc * lax.rsqrt(var + LN_EPS)


def _gelu_tanh(x):
    return 0.5 * x * (1.0 + jnp.tanh(0.7978845608028654 * (x + 0.044715 * (x * x * x))))


def _rope(t, cos, sin_signed):
    lane = lax.broadcasted_iota(jnp.int32, t.shape, 1)
    partner = jnp.where((lane & ROPE_PAIRS) == 0,
                        pltpu.roll(t, LANES - ROPE_PAIRS, 1), pltpu.roll(t, ROPE_PAIRS, 1))
    return t * cos + partner * sin_signed


def _resident(shape):
    zeros = (0,) * len(shape)
    return pl.BlockSpec(shape, lambda *_: zeros, pipeline_mode=pl.Buffered(1))


def _ada_kernel(c_ref, w_ref, b_ref, o_ref):
    c = c_ref[...]
    s = (c * jax.nn.sigmoid(c)).astype(BF16)
    o_ref[...] = jnp.dot(s, w_ref[...].astype(BF16), preferred_element_type=F32) + b_ref[...]


def _ada_rows(c_rows, w_ada, b_ada):
    n = w_ada.shape[1]
    return pl.pallas_call(
        _ada_kernel,
        grid=(n // ADA_BLOCK_N,),
        in_specs=[pl.BlockSpec(c_rows.shape, lambda j: (0, 0)),
                  pl.BlockSpec((D_MODEL, ADA_BLOCK_N), lambda j: (0, j)),
                  pl.BlockSpec((1, ADA_BLOCK_N), lambda j: (0, j))],
        out_specs=pl.BlockSpec((c_rows.shape[0], ADA_BLOCK_N), lambda j: (0, j)),
        out_shape=jax.ShapeDtypeStruct((c_rows.shape[0], n), F32),
        name="ada_rows",
    )(c_rows, w_ada, b_ada.reshape(1, n))


def _kv_kernel(*refs, rope):
    if rope:
        x_ref, mod_ref, w_ref, cos_ref, sin_ref, k_ref, vt_ref = refs
    else:
        x_ref, mod_ref, w_ref, k_ref, vt_ref = refs
    m = mod_ref[0]
    h = _layer_norm(x_ref[0]) * (1.0 + m[1:2]) + m[0:1]
    kv = jnp.dot(h.astype(BF16), w_ref[...], preferred_element_type=F32)
    k, vt = kv[:, :KV_W], kv[:, KV_W:].T
    if rope:
        k = _rope(k, cos_ref[...], sin_ref[...])
    for g in range(N_KV_HEADS):
        k_ref[0, g] = k[:, g * HEAD_DIM:(g + 1) * HEAD_DIM].astype(BF16)
        vt_ref[0, g] = vt[g * HEAD_DIM:(g + 1) * HEAD_DIM, :].astype(BF16)


def _kv_project(x, mod, mod_row, w_in_b, tables, tile):
    bsz, t, _ = x.shape
    rope = tables is not None
    in_specs = [pl.BlockSpec((1, tile, D_MODEL), lambda b, i: (b, i, 0)),
                pl.BlockSpec((1, 6, D_MODEL), lambda b, i: (mod_row(b), 0, 0)),
                pl.BlockSpec((D_MODEL, 2 * KV_W), lambda b, i: (0, OFF_K // (2 * KV_W)))]
    args = [x, mod, w_in_b]
    if rope:
        in_specs += [pl.BlockSpec((tile, LANES), lambda b, i: (i, 0))] * 2
        args += list(tables)
    return pl.pallas_call(
        functools.partial(_kv_kernel, rope=rope),
        grid=(bsz, t // tile),
        in_specs=in_specs,
        out_specs=[pl.BlockSpec((1, N_KV_HEADS, tile, HEAD_DIM), lambda b, i: (b, 0, i, 0)),
                   pl.BlockSpec((1, N_KV_HEADS, HEAD_DIM, tile), lambda b, i: (b, 0, 0, i))],
        out_shape=[jax.ShapeDtypeStruct((bsz, N_KV_HEADS, t, HEAD_DIM), BF16),
                   jax.ShapeDtypeStruct((bsz, N_KV_HEADS, HEAD_DIM, t), BF16)],
        name="kv_latent" if rope else "kv_context",
    )(*args)


def _attn_scores(q_blk, kk, bias_prev, bias_next, sink_row):
    n_ctx = kk.shape[0] - 3 * BLOCK
    st = lax.dot_general(kk, q_blk, (((1,), (1,)), ((), ())), preferred_element_type=F32)
    slabs = [st[:n_ctx],
             st[n_ctx:n_ctx + BLOCK] + bias_prev,
             st[n_ctx + BLOCK:n_ctx + 2 * BLOCK],
             st[n_ctx + 2 * BLOCK:] + bias_next]
    mx = sink_row
    for sl in slabs:
        mx = jnp.maximum(mx, jnp.max(sl, axis=0, keepdims=True))
    return slabs, mx


def _attn_output(slabs, mx, vvt, sink_row):
    den = jnp.exp2(sink_row - mx)
    probs = []
    for sl in slabs:
        p = jnp.exp2(sl - mx)
        den = den + jnp.sum(p, axis=0, keepdims=True)
        probs.append(p.astype(BF16))
    out_t = jnp.dot(vvt, jnp.concatenate(probs, axis=0), preferred_element_type=F32)
    return out_t / den


def _mixer_kernel(x_ref, mod_ref, cos_ref, sin_ref, sink_ref,
                  kp_ref, km_ref, kn_ref, vp_ref, vm_ref, vn_ref, kc_ref, vc_ref,
                  win_ref, glg_ref, glb_ref, ws_ref, bs_ref, wa_ref, wb_ref, wo_ref,
                  l1g_ref, l1b_ref, o_ref, *, tile, n_tiles):
    i = pl.program_id(1)
    n_blk = tile // BLOCK
    x = x_ref[0]
    m = mod_ref[0]

    part = tile // ROW_PARTS
    h_parts, q_parts = [], []
    for r in range(ROW_PARTS):
        rows = slice(r * part, (r + 1) * part)
        h_parts.append((_layer_norm(x[rows]) * (1.0 + m[1:2]) + m[0:1]).astype(BF16))
        q_parts.append(jnp.dot(h_parts[r], win_ref[:, OFF_Q:OFF_Q + Q_W], preferred_element_type=F32))
    h = jnp.concatenate(h_parts, axis=0)

    def proj(off, width):
        return jnp.dot(h, win_ref[:, off:off + width], preferred_element_type=F32)

    piece_w = 2 * LANES
    n_pieces = (IN_W - OFF_U) // piece_w
    pieces = []

    def emit_pieces(upto):
        while len(pieces) < upto:
            pieces.append(proj(OFF_U + len(pieces) * piece_w, piece_w))

    emit_pieces(HEAD_PIECES)

    q = jnp.concatenate(q_parts, axis=0)
    cos, sin = cos_ref[...], sin_ref[...]
    q = jnp.concatenate(
        [(_rope(q[:, c * LANES:(c + 1) * LANES], cos, sin) * Q_SCALE).astype(BF16)
         for c in range(Q_W // LANES)], axis=1)

    kj = lax.broadcasted_iota(jnp.int32, (BLOCK, GQA_GROUP * BLOCK), 0)
    qi = lax.broadcasted_iota(jnp.int32, (BLOCK, GQA_GROUP * BLOCK), 1) & (BLOCK - 1)
    band_prev = jnp.where(kj >= qi, 0.0, NEG_INF)
    band_next = jnp.where(kj <= qi, 0.0, NEG_INF)
    first_prev = jnp.where(i == 0, NEG_INF, band_prev)
    last_next = jnp.where(i == n_tiles - 1, NEG_INF, band_next)

    k_ext =[jnp.concatenate([kp_ref[0, g], km_ref[0, g], kn_ref[0, g]], axis=0) for g in range(N_KV_HEADS)]
    vt_ext = [jnp.concatenate([vp_ref[0, g], vm_ref[0, g], vn_ref[0, g]], axis=1) for g in range(N_KV_HEADS)]
    sink_rows = [sink_ref[g:g + 1, :] * LOG2E for g in range(N_KV_HEADS)]
    head_out = [[None] * n_blk for _ in range(N_Q_HEADS)]

    def scores(g, j):
        rows = slice(j * BLOCK, (j + 1) * BLOCK)
        q_blk = jnp.concatenate(
            [q[rows, (g * GQA_GROUP + hh) * HEAD_DIM:(g * GQA_GROUP + hh + 1) * HEAD_DIM]
             for hh in range(GQA_GROUP)], axis=0)
        kk = jnp.concatenate([kc_ref[0, g], k_ext[g][j * BLOCK:(j + 3) * BLOCK]], axis=0)
        return _attn_scores(q_blk, kk, first_prev if j == 0 else band_prev,
                            last_next if j == n_blk - 1 else band_next, sink_rows[g])

    def finish(g, j, slabs, mx):
        vvt = jnp.concatenate([vc_ref[0, g], vt_ext[g][:, j * BLOCK:(j + 3) * BLOCK]], axis=1)
        out_t = _attn_output(slabs, mx, vvt, sink_rows[g])
        for hh in range(GQA_GROUP):
            head_out[g * GQA_GROUP + hh][j] = out_t[:, hh * BLOCK:(hh + 1) * BLOCK]

    units = [(g, j) for g in range(N_KV_HEADS) for j in range(n_blk)]
    pending = None
    for t, (g, j) in enumerate(units):
        slabs, mx = scores(g, j)
        if pending is not None:
            finish(*pending)
        pending = (g, j, slabs, mx)
        emit_pieces(HEAD_PIECES + -(-(n_pieces - HEAD_PIECES) * (t + 1) // len(units)))
    finish(*pending)
    dense = jnp.concatenate(pieces, axis=1)

    def dense_cols(off, width):
        return dense[:, off - OFF_U:off - OFF_U + width]

    ya_t =jnp.concatenate([jnp.concatenate(head_out[hd], axis=1) for hd in range(N_Q_HEADS)], axis=0)
    a_proj = jnp.dot(ya_t.T.astype(BF16), wa_ref[...], preferred_element_type=F32)

    u = _gelu_tanh(dense_cols(OFF_U, GMLP_WIDTH))
    vn = (_layer_norm(_gelu_tanh(dense_cols(OFF_VB, GMLP_WIDTH))) * glg_ref[...] + glb_ref[...]).astype(BF16)
    bias = bs_ref[...]
    gated = []
    for j in range(n_blk):
        rows = slice(j * BLOCK, (j + 1) * BLOCK)
        s_chunk = jnp.concatenate(
            [jnp.dot(ws_ref[gi], vn[rows, gi * GMLP_GROUP_DIM:(gi + 1) * GMLP_GROUP_DIM],
                     preferred_element_type=F32) for gi in range(N_GMLP_GROUPS)], axis=1)
        gated.append(u[rows] * (s_chunk + bias))
    yb = jnp.concatenate(gated, axis=0)
    b_proj = jnp.dot(yb.astype(BF16), wb_ref[...], preferred_element_type=F32)

    merged = (jax.nn.sigmoid(dense_cols(OFF_GA, D_MODEL)) * a_proj
              + jax.nn.sigmoid(dense_cols(OFF_GB, D_MODEL)) * b_proj)
    merged = merged.astype(BF16)
    part = tile // ROW_PARTS
    mixes = [jnp.dot(merged[r * part:(r + 1) * part], wo_ref[...], preferred_element_type=F32)
             for r in range(ROW_PARTS)]
    for r, mix in enumerate(mixes):
        rows = slice(r * part, (r + 1) * part)
        o_ref[0, rows] = (_layer_norm(DEEPNORM_ALPHA * x[rows] + m[2:3] * mix) * l1g_ref[...]
                          + l1b_ref[...])


def _mixer(x, mod, sink_rows, tables, k, vt, kc, vct, w_in_b, glg, glb, ws_b, bs_full, wa_b, wb_b, wo_b,
           l1g, l1b):
    bsz, seq, _ = x.shape
    tile = MIX_TILE
    n_tiles = seq // tile
    bpt = tile // BLOCK
    n_ctx = kc.shape[2]
    last_blk = seq // BLOCK - 1

    def prev_blk(i):
        return jnp.maximum(i * bpt - 1, 0)

    def next_blk(i):
        return jnp.minimum((i + 1) * bpt, last_blk)

    x_spec = pl.BlockSpec((1, tile, D_MODEL), lambda b, i: (b, i, 0))
    table_spec = pl.BlockSpec((tile, LANES), lambda b, i: (i, 0))
    k_halo = (1, N_KV_HEADS, BLOCK, HEAD_DIM)
    vt_halo = (1, N_KV_HEADS, HEAD_DIM, BLOCK)
    in_specs = [
        x_spec,
        pl.BlockSpec((1, 6, D_MODEL), lambda b, i: (b, 0, 0)),
        table_spec, table_spec,
        _resident(sink_rows.shape),
        pl.BlockSpec(k_halo, lambda b, i: (b, 0, prev_blk(i), 0)),
        pl.BlockSpec((1, N_KV_HEADS, tile, HEAD_DIM), lambda b, i: (b, 0, i, 0)),
        pl.BlockSpec(k_halo, lambda b, i: (b, 0, next_blk(i), 0)),
        pl.BlockSpec(vt_halo, lambda b, i: (b, 0, 0, prev_blk(i))),
        pl.BlockSpec((1, N_KV_HEADS, HEAD_DIM, tile), lambda b, i: (b, 0, 0, i)),
        pl.BlockSpec(vt_halo, lambda b, i: (b, 0, 0, next_blk(i))),
        pl.BlockSpec((1, N_KV_HEADS, n_ctx, HEAD_DIM), lambda b, i: (b, 0, 0, 0)),
        pl.BlockSpec((1, N_KV_HEADS, HEAD_DIM, n_ctx), lambda b, i: (b, 0, 0, 0)),
        _resident(w_in_b.shape), _resident(glg.shape), _resident(glb.shape), _resident(ws_b.shape),
        _resident(bs_full.shape), _resident(wa_b.shape), _resident(wb_b.shape), _resident(wo_b.shape),
        _resident(l1g.shape), _resident(l1b.shape),
    ]
    return pl.pallas_call(
        functools.partial(_mixer_kernel, tile=tile, n_tiles=n_tiles),
        grid=(bsz, n_tiles),
        in_specs=in_specs,
        out_specs=x_spec,
        out_shape=jax.ShapeDtypeStruct(x.shape, F32),
        compiler_params=pltpu.CompilerParams(
            dimension_semantics=("arbitrary", "arbitrary"), vmem_limit_bytes=VMEM_LIMIT),
        name="mixer",
    )(x, mod, tables[0], tables[1], sink_rows, k, k, k, vt, vt, vt, kc, vct,
      w_in_b, glg, glb, ws_b, bs_full, wa_b, wb_b, wo_b, l1g, l1b)


def _ffn_kernel(x_ref, mod_ref, w1_ref, w2_ref, l2g_ref, l2b_ref, o_ref):
    x = x_ref[0]
    m = mod_ref[0]
    h = (_layer_norm(x) * (1.0 + m[4:5]) + m[3:4]).astype(BF16)
    acc = None
    for c in range(FFN_HIDDEN // FFN_CHUNK):
        lo = c * FFN_CHUNK
        gate = jnp.dot(h, w1_ref[:, lo:lo + FFN_CHUNK], preferred_element_type=F32)
        up = jnp.dot(h, w1_ref[:, FFN_HIDDEN + lo:FFN_HIDDEN + lo + FFN_CHUNK], preferred_element_type=F32)
        act = (gate * jax.nn.sigmoid(gate) * up).astype(BF16)
        part = jnp.dot(act, w2_ref[lo:lo + FFN_CHUNK, :], preferred_element_type=F32)
        acc = part if acc is None else acc + part
    o_ref[0] = _layer_norm(DEEPNORM_ALPHA * x + m[5:6] * acc) * l2g_ref[...] + l2b_ref[...]


def _ffn(x, mod, w1_b, w2_b, l2g, l2b):
    bsz, seq, _ = x.shape
    tile = FFN_TILE
    x_spec = pl.BlockSpec((1, tile, D_MODEL), lambda b, i: (b, i, 0))
    return pl.pallas_call(
        _ffn_kernel,
        grid=(bsz, seq // tile),
        in_specs=[x_spec, pl.BlockSpec((1, 6, D_MODEL), lambda b, i: (b, 0, 0)),
                  _resident(w1_b.shape), _resident(w2_b.shape),
                  _resident(l2g.shape), _resident(l2b.shape)],
        out_specs=x_spec,
        out_shape=jax.ShapeDtypeStruct(x.shape, F32),
        compiler_params=pltpu.CompilerParams(
            dimension_semantics=("arbitrary", "arbitrary"), vmem_limit_bytes=VMEM_LIMIT),
        name="ffn",
    )(x, mod, w1_b, w2_b, l2g, l2b)


def _rope_tables(seq):
    inv = ROPE_BASE ** (-jnp.arange(ROPE_PAIRS, dtype=F32) / ROPE_PAIRS)
    t = jnp.arange(seq, dtype=jnp.int32)
    ar = (t // GRID_W).astype(F32)[:, None] * inv
    ac = (t % GRID_W).astype(F32)[:, None] * inv
    cos = jnp.concatenate([jnp.cos(ar), jnp.cos(ar), jnp.cos(ac), jnp.cos(ac)], axis=1)
    sin = jnp.concatenate([-jnp.sin(ar), jnp.sin(ar), -jnp.sin(ac), jnp.sin(ac)], axis=1)
    reps = LANES // HEAD_DIM
    return jnp.tile(cos, (1, reps)), jnp.tile(sin, (1, reps))


def kernel(x, c, ctx, c_ctx, w_ada, b_ada, w_in, attn_sink, gmlp_ln_g, gmlp_ln_b, w_spatial, b_spatial,
           w_branch_a, w_branch_b, w_out, ln1_g, ln1_b, w_ffn_in, w_ffn_out, ln2_g, ln2_b):
    bsz, seq, _ = x.shape
    n_ctx = ctx.shape[1]
    assert seq % MIX_TILE == 0 and seq % KV_TILE == 0 and seq % FFN_TILE == 0 and seq % GRID_W == 0
    assert w_ada.shape[0] == DEPTH

    pad = (-(bsz + 1)) % 8
    c_rows = jnp.concatenate([c, c_ctx[None], jnp.zeros((pad, D_MODEL), F32)], axis=0)
    mod = _ada_rows(c_rows, w_ada[0], b_ada[0]).reshape(c_rows.shape[0], 6, D_MODEL)

    w_in_b = w_in[0].astype(BF16)
    bs_full = jnp.repeat(b_spatial[0].T, GMLP_GROUP_DIM, axis=1)
    sink_rows = jnp.repeat(attn_sink[0], BLOCK).reshape(N_KV_HEADS, GQA_GROUP * BLOCK)
    row = lambda a: a[0].reshape(1, -1)

    tables = _rope_tables(seq)
    k, vt = _kv_project(x, mod, lambda b: b, w_in_b, tables, KV_TILE)
    kc, vct = _kv_project(ctx, mod, lambda b: bsz, w_in_b, None, n_ctx)

    x_mid = _mixer(x, mod, sink_rows, tables, k, vt, kc, vct, w_in_b,
                   row(gmlp_ln_g), row(gmlp_ln_b), w_spatial[0].astype(BF16), bs_full,
                   w_branch_a[0].astype(BF16), w_branch_b[0].astype(BF16), w_out[0].astype(BF16),
                   row(ln1_g), row(ln1_b))
    return _ffn(x_mid, mod, w_ffn_in[0].astype(BF16), w_ffn_out[0].astype(BF16), row(ln2_g), row(ln2_b))
```

```python
import functools
import math

import jax
import jax.numpy as jnp
from jax import lax
from jax.experimental import pallas as pl
from jax.experimental.pallas import tpu as pltpu

F32 = jnp.float32
BF16 = jnp.bfloat16

D_MODEL = 1024
GRID_W = 64
HEAD_DIM = 64
N_Q_HEADS = 8
N_KV_HEADS = 2
GQA_GROUP = N_Q_HEADS // N_KV_HEADS
BLOCK = 128
ROPE_BASE = 10000.0
ROPE_PAIRS = HEAD_DIM // 4
N_GMLP_GROUPS = 8
GMLP_GROUP_DIM = 64
GMLP_WIDTH = N_GMLP_GROUPS * GMLP_GROUP_DIM
FFN_HIDDEN = 2816
Q_W = N_Q_HEADS * HEAD_DIM
KV_W = N_KV_HEADS * HEAD_DIM
OFF_Q, OFF_K, OFF_U = 0, Q_W, Q_W + 2 * KV_W
OFF_VB = OFF_U + GMLP_WIDTH
OFF_GA = OFF_VB + GMLP_WIDTH
OFF_GB = OFF_GA + D_MODEL
IN_W = OFF_GB + D_MODEL
LN_EPS = 1e-5
NEG_INF = -1e30
DEPTH = 1
DEEPNORM_ALPHA = (2 * DEPTH) ** 0.25
LOG2E = math.log2(math.e)
Q_SCALE = HEAD_DIM ** -0.5 * LOG2E

LANES = 128
ADA_BLOCK_N = 1536
KV_TILE = 512
MIX_TILE = 512
FFN_TILE = 1024
FFN_PARTS = 2
FFN_CHUNK = 256
ROW_PARTS = 2
HEAD_PIECES = 2
VMEM_LIMIT = 56 * 1024 * 1024


def _layer_norm(x):
    mu = jnp.mean(x, axis=-1, keepdims=True)
    xc = x - mu
    var = jnp.mean(xc * xc, axis=-1, keepdims=True)
    return xc * lax.rsqrt(var + LN_EPS)


def _gelu_tanh(x):
    return 0.5 * x * (1.0 + jnp.tanh(0.7978845608028654 * (x + 0.044715 * (x * x * x))))


def _rope(t, cos, sin_signed):
    lane = lax.broadcasted_iota(jnp.int32, t.shape, 1)
    partner = jnp.where((lane & ROPE_PAIRS) == 0,
                        pltpu.roll(t, LANES - ROPE_PAIRS, 1), pltpu.roll(t, ROPE_PAIRS, 1))
    return t * cos + partner * sin_signed


def _resident(shape):
    zeros = (0,) * len(shape)
    return pl.BlockSpec(shape, lambda *_: zeros, pipeline_mode=pl.Buffered(1))


def _ada_kernel(c_ref, w_ref, b_ref, o_ref):
    c = c_ref[...]
    s = (c * jax.nn.sigmoid(c)).astype(BF16)
    o_ref[...] = jnp.dot(s, w_ref[...].astype(BF16), preferred_element_type=F32) + b_ref[...]


def _ada_rows(c_rows, w_ada, b_ada):
    n = w_ada.shape[1]
    return pl.pallas_call(
        _ada_kernel,
        grid=(n // ADA_BLOCK_N,),
        in_specs=[pl.BlockSpec(c_rows.shape, lambda j: (0, 0)),
                  pl.BlockSpec((D_MODEL, ADA_BLOCK_N), lambda j: (0, j)),
                  pl.BlockSpec((1, ADA_BLOCK_N), lambda j: (0, j))],
        out_specs=pl.BlockSpec((c_rows.shape[0], ADA_BLOCK_N), lambda j: (0, j)),
        out_shape=jax.ShapeDtypeStruct((c_rows.shape[0], n), F32),
        name="ada_rows",
    )(c_rows, w_ada, b_ada.reshape(1, n))


def _kv_kernel(*refs, rope):
    if rope:
        x_ref, mod_ref, w_ref, cos_ref, sin_ref, k_ref, vt_ref = refs
    else:
        x_ref, mod_ref, w_ref, k_ref, vt_ref = refs
    m = mod_ref[0]
    h = _layer_norm(x_ref[0]) * (1.0 + m[1:2]) + m[0:1]
    kv = jnp.dot(h.astype(BF16), w_ref[...], preferred_element_type=F32)
    k, vt = kv[:, :KV_W], kv[:, KV_W:].T
    if rope:
        k = _rope(k, cos_ref[...], sin_ref[...])
    for g in range(N_KV_HEADS):
        k_ref[0, g] = k[:, g * HEAD_DIM:(g + 1) * HEAD_DIM].astype(BF16)
        vt_ref[0, g] = vt[g * HEAD_DIM:(g + 1) * HEAD_DIM, :].astype(BF16)


def _kv_project(x, mod, mod_row, w_in_b, tables, tile):
    bsz, t, _ = x.shape
    rope = tables is not None
    in_specs = [pl.BlockSpec((1, tile, D_MODEL), lambda b, i: (b, i, 0)),
                pl.BlockSpec((1, 6, D_MODEL), lambda b, i: (mod_row(b), 0, 0)),
                pl.BlockSpec((D_MODEL, 2 * KV_W), lambda b, i: (0, OFF_K // (2 * KV_W)))]
    args = [x, mod, w_in_b]
    if rope:
        in_specs += [pl.BlockSpec((tile, LANES), lambda b, i: (i, 0))] * 2
        args += list(tables)
    return pl.pallas_call(
        functools.partial(_kv_kernel, rope=rope),
        grid=(bsz, t // tile),
        in_specs=in_specs,
        out_specs=[pl.BlockSpec((1, N_KV_HEADS, tile, HEAD_DIM), lambda b, i: (b, 0, i, 0)),
                   pl.BlockSpec((1, N_KV_HEADS, HEAD_DIM, tile), lambda b, i: (b, 0, 0, i))],
        out_shape=[jax.ShapeDtypeStruct((bsz, N_KV_HEADS, t, HEAD_DIM), BF16),
                   jax.ShapeDtypeStruct((bsz, N_KV_HEADS, HEAD_DIM, t), BF16)],
        name="kv_latent" if rope else "kv_context",
    )(*args)


def _attn_scores(q_blk, kk, bias_prev, bias_next, sink_row):
    n_ctx = kk.shape[0] - 3 * BLOCK
    st = lax.dot_general(kk, q_blk, (((1,), (1,)), ((), ())), preferred_element_type=F32)
    slabs = [st[:n_ctx],
             st[n_ctx:n_ctx + BLOCK] + bias_prev,
             st[n_ctx + BLOCK:n_ctx + 2 * BLOCK],
             st[n_ctx + 2 * BLOCK:] + bias_next]
    mx = sink_row
    for sl in slabs:
        mx = jnp.maximum(mx, jnp.max(sl, axis=0, keepdims=True))
    return slabs, mx


def _attn_output(slabs, mx, vvt, sink_row):
    den = jnp.exp2(sink_row - mx)
    probs = []
    for sl in slabs:
        p = jnp.exp2(sl - mx)
        den = den + jnp.sum(p, axis=0, keepdims=True)
        probs.append(p.astype(BF16))
    out_t = jnp.dot(vvt, jnp.concatenate(probs, axis=0), preferred_element_type=F32)
    return out_t / den


def _mixer_kernel(x_ref, mod_ref, cos_ref, sin_ref, sink_ref,
                  kp_ref, km_ref, kn_ref, vp_ref, vm_ref, vn_ref, kc_ref, vc_ref,
                  win_ref, glg_ref, glb_ref, ws_ref, bs_ref, wa_ref, wb_ref, wo_ref,
                  l1g_ref, l1b_ref, o_ref, *, tile, n_tiles):
    i = pl.program_id(1)
    n_blk = tile // BLOCK
    x = x_ref[0]
    m = mod_ref[0]

    part = tile // ROW_PARTS
    h_parts, q_parts = [], []
    for r in range(ROW_PARTS):
        rows = slice(r * part, (r + 1) * part)
        h_parts.append((_layer_norm(x[rows]) * (1.0 + m[1:2]) + m[0:1]).astype(BF16))
        q_parts.append(jnp.dot(h_parts[r], win_ref[:, OFF_Q:OFF_Q + Q_W], preferred_element_type=F32))
    h = jnp.concatenate(h_parts, axis=0)

    def proj(off, width):
        return jnp.dot(h, win_ref[:, off:off + width], preferred_element_type=F32)

    piece_w = 2 * LANES
    n_pieces = (IN_W - OFF_U) // piece_w
    pieces = []

    def emit_pieces(upto):
        while len(pieces) < upto:
            pieces.append(proj(OFF_U + len(pieces) * piece_w, piece_w))

    emit_pieces(HEAD_PIECES)

    q = jnp.concatenate(q_parts, axis=0)
    cos, sin = cos_ref[...], sin_ref[...]
    q = jnp.concatenate(
        [(_rope(q[:, c * LANES:(c + 1) * LANES], cos, sin) * Q_SCALE).astype(BF16)
         for c in range(Q_W // LANES)], axis=1)

    kj = lax.broadcasted_iota(jnp.int32, (BLOCK, GQA_GROUP * BLOCK), 0)
    qi = lax.broadcasted_iota(jnp.int32, (BLOCK, GQA_GROUP * BLOCK), 1) & (BLOCK - 1)
    band_prev = jnp.where(kj >= qi, 0.0, NEG_INF)
    band_next = jnp.where(kj <= qi, 0.0, NEG_INF)
    first_prev = jnp.where(i == 0, NEG_INF, band_prev)
    last_next = jnp.where(i == n_tiles - 1, NEG_INF, band_next)

    k_ext =[jnp.concatenate([kp_ref[0, g], km_ref[0, g], kn_ref[0, g]], axis=0) for g in range(N_KV_HEADS)]
    vt_ext = [jnp.concatenate([vp_ref[0, g], vm_ref[0, g], vn_ref[0, g]], axis=1) for g in range(N_KV_HEADS)]
    sink_rows = [sink_ref[g:g + 1, :] * LOG2E for g in range(N_KV_HEADS)]
    head_out = [[None] * n_blk for _ in range(N_Q_HEADS)]

    def scores(g, j):
        rows = slice(j * BLOCK, (j + 1) * BLOCK)
        q_blk = jnp.concatenate(
            [q[rows, (g * GQA_GROUP + hh) * HEAD_DIM:(g * GQA_GROUP + hh + 1) * HEAD_DIM]
             for hh in range(GQA_GROUP)], axis=0)
        kk = jnp.concatenate([kc_ref[0, g], k_ext[g][j * BLOCK:(j + 3) * BLOCK]], axis=0)
        return _attn_scores(q_blk, kk, first_prev if j == 0 else band_prev,
                            last_next if j == n_blk - 1 else band_next, sink_rows[g])

    def finish(g, j, slabs, mx):
        vvt = jnp.concatenate([vc_ref[0, g], vt_ext[g][:, j * BLOCK:(j + 3) * BLOCK]], axis=1)
        out_t = _attn_output(slabs, mx, vvt, sink_rows[g])
        for hh in range(GQA_GROUP):
            head_out[g * GQA_GROUP + hh][j] = out_t[:, hh * BLOCK:(hh + 1) * BLOCK]

    units = [(g, j) for g in range(N_KV_HEADS) for j in range(n_blk)]
    pending = None
    for t, (g, j) in enumerate(units):
        slabs, mx = scores(g, j)
        if pending is not None:
            finish(*pending)
        pending = (g, j, slabs, mx)
        emit_pieces(HEAD_PIECES + -(-(n_pieces - HEAD_PIECES) * (t + 1) // len(units)))
    finish(*pending)
    dense = jnp.concatenate(pieces, axis=1)

    def dense_cols(off, width):
        return dense[:, off - OFF_U:off - OFF_U + width]

    ya_t =jnp.concatenate([jnp.concatenate(head_out[hd], axis=1) for hd in range(N_Q_HEADS)], axis=0)
    a_proj = jnp.dot(ya_t.T.astype(BF16), wa_ref[...], preferred_element_type=F32)

    u = _gelu_tanh(dense_cols(OFF_U, GMLP_WIDTH))
    vn = (_layer_norm(_gelu_tanh(dense_cols(OFF_VB, GMLP_WIDTH))) * glg_ref[...] + glb_ref[...]).astype(BF16)
    bias = bs_ref[...]
    gated = []
    for j in range(n_blk):
        rows = slice(j * BLOCK, (j + 1) * BLOCK)
        s_chunk = jnp.concatenate(
            [jnp.dot(ws_ref[gi], vn[rows, gi * GMLP_GROUP_DIM:(gi + 1) * GMLP_GROUP_DIM],
                     preferred_element_type=F32) for gi in range(N_GMLP_GROUPS)], axis=1)
        gated.append(u[rows] * (s_chunk + bias))
    yb = jnp.concatenate(gated, axis=0)
    b_proj = jnp.dot(yb.astype(BF16), wb_ref[...], preferred_element_type=F32)

    merged = (jax.nn.sigmoid(dense_cols(OFF_GA, D_MODEL)) * a_proj
              + jax.nn.sigmoid(dense_cols(OFF_GB, D_MODEL)) * b_proj)
    merged = merged.astype(BF16)
    part = tile // ROW_PARTS
    mixes = [jnp.dot(merged[r * part:(r + 1) * part], wo_ref[...], preferred_element_type=F32)
             for r in range(ROW_PARTS)]
    for r, mix in enumerate(mixes):
        rows = slice(r * part, (r + 1) * part)
        o_ref[0, rows] = (_layer_norm(DEEPNORM_ALPHA * x[rows] + m[2:3] * mix) * l1g_ref[...]
                          + l1b_ref[...])


def _mixer(x, mod, sink_rows, tables, k, vt, kc, vct, w_in_b, glg, glb, ws_b, bs_full, wa_b, wb_b, wo_b,
           l1g, l1b):
    bsz, seq, _ = x.shape
    tile = MIX_TILE
    n_tiles = seq // tile
    bpt = tile // BLOCK
    n_ctx = kc.shape[2]
    last_blk = seq // BLOCK - 1

    def prev_blk(i):
        return jnp.maximum(i * bpt - 1, 0)

    def next_blk(i):
        return jnp.minimum((i + 1) * bpt, last_blk)

    x_spec = pl.BlockSpec((1, tile, D_MODEL), lambda b, i: (b, i, 0))
    table_spec = pl.BlockSpec((tile, LANES), lambda b, i: (i, 0))
    k_halo = (1, N_KV_HEADS, BLOCK, HEAD_DIM)
    vt_halo = (1, N_KV_HEADS, HEAD_DIM, BLOCK)
    in_specs = [
        x_spec,
        pl.BlockSpec((1, 6, D_MODEL), lambda b, i: (b, 0, 0)),
        table_spec, table_spec,
        _resident(sink_rows.shape),
        pl.BlockSpec(k_halo, lambda b, i: (b, 0, prev_blk(i), 0)),
        pl.BlockSpec((1, N_KV_HEADS, tile, HEAD_DIM), lambda b, i: (b, 0, i, 0)),
        pl.BlockSpec(k_halo, lambda b, i: (b, 0, next_blk(i), 0)),
        pl.BlockSpec(vt_halo, lambda b, i: (b, 0, 0, prev_blk(i))),
        pl.BlockSpec((1, N_KV_HEADS, HEAD_DIM, tile), lambda b, i: (b, 0, 0, i)),
        pl.BlockSpec(vt_halo, lambda b, i: (b, 0, 0, next_blk(i))),
        pl.BlockSpec((1, N_KV_HEADS, n_ctx, HEAD_DIM), lambda b, i: (b, 0, 0, 0)),
        pl.BlockSpec((1, N_KV_HEADS, HEAD_DIM, n_ctx), lambda b, i: (b, 0, 0, 0)),
        _resident(w_in_b.shape), _resident(glg.shape), _resident(glb.shape), _resident(ws_b.shape),
        _resident(bs_full.shape), _resident(wa_b.shape), _resident(wb_b.shape), _resident(wo_b.shape),
        _resident(l1g.shape), _resident(l1b.shape),
    ]
    return pl.pallas_call(
        functools.partial(_mixer_kernel, tile=tile, n_tiles=n_tiles),
        grid=(bsz, n_tiles),
        in_specs=in_specs,
        out_specs=x_spec,
        out_shape=jax.ShapeDtypeStruct(x.shape, F32),
        compiler_params=pltpu.CompilerParams(
            dimension_semantics=("arbitrary", "arbitrary"), vmem_limit_bytes=VMEM_LIMIT),
        name="mixer",
    )(x, mod, tables[0], tables[1], sink_rows, k, k, k, vt, vt, vt, kc, vct,
      w_in_b, glg, glb, ws_b, bs_full, wa_b, wb_b, wo_b, l1g, l1b)


def _ffn_kernel(x_ref, mod_ref, w1_ref, w2_ref, l2g_ref, l2b_ref, o_ref):
    m = mod_ref[0]
    part = x_ref.shape[1] // FFN_PARTS
    rows = [slice(r * part, (r + 1) * part) for r in range(FFN_PARTS)]
    h = [None] * FFN_PARTS
    acc = [None] * FFN_PARTS
    n_chunk = FFN_HIDDEN // FFN_CHUNK
    for c in range(n_chunk):
        for r in range(FFN_PARTS):
            if c == 0:
                h[r] = (_layer_norm(x_ref[0, rows[r]]) * (1.0 + m[4:5]) + m[3:4]).astype(BF16)
            lo = c * FFN_CHUNK
            gate = jnp.dot(h[r], w1_ref[:, lo:lo + FFN_CHUNK], preferred_element_type=F32)
            up = jnp.dot(h[r], w1_ref[:, FFN_HIDDEN + lo:FFN_HIDDEN + lo + FFN_CHUNK],
                         preferred_element_type=F32)
            act = (gate * jax.nn.sigmoid(gate) * up).astype(BF16)
            down = jnp.dot(act, w2_ref[lo:lo + FFN_CHUNK, :], preferred_element_type=F32)
            acc[r] = down if acc[r] is None else acc[r] + down
            if c == n_chunk - 1:
                o_ref[0, rows[r]] = (_layer_norm(DEEPNORM_ALPHA * x_ref[0, rows[r]] + m[5:6] * acc[r])
                                     * l2g_ref[...] + l2b_ref[...])


def _ffn(x, mod, w1_b, w2_b, l2g, l2b):
    bsz, seq, _ = x.shape
    tile = FFN_TILE
    x_spec = pl.BlockSpec((1, tile, D_MODEL), lambda b, i: (b, i, 0))
    return pl.pallas_call(
        _ffn_kernel,
        grid=(bsz, seq // tile),
        in_specs=[x_spec, pl.BlockSpec((1, 6, D_MODEL), lambda b, i: (b, 0, 0)),
                  _resident(w1_b.shape), _resident(w2_b.shape),
                  _resident(l2g.shape), _resident(l2b.shape)],
        out_specs=x_spec,
        out_shape=jax.ShapeDtypeStruct(x.shape, F32),
        compiler_params=pltpu.CompilerParams(
            dimension_semantics=("arbitrary", "arbitrary"), vmem_limit_bytes=VMEM_LIMIT),
        name="ffn",
    )(x, mod, w1_b, w2_b, l2g, l2b)


def _rope_tables(seq):
    inv = ROPE_BASE ** (-jnp.arange(ROPE_PAIRS, dtype=F32) / ROPE_PAIRS)
    t = jnp.arange(seq, dtype=jnp.int32)
    ar = (t // GRID_W).astype(F32)[:, None] * inv
    ac = (t % GRID_W).astype(F32)[:, None] * inv
    cos = jnp.concatenate([jnp.cos(ar), jnp.cos(ar), jnp.cos(ac), jnp.cos(ac)], axis=1)
    sin = jnp.concatenate([-jnp.sin(ar), jnp.sin(ar), -jnp.sin(ac), jnp.sin(ac)], axis=1)
    reps = LANES // HEAD_DIM
    return jnp.tile(cos, (1, reps)), jnp.tile(sin, (1, reps))


def kernel(x, c, ctx, c_ctx, w_ada, b_ada, w_in, attn_sink, gmlp_ln_g, gmlp_ln_b, w_spatial, b_spatial,
           w_branch_a, w_branch_b, w_out, ln1_g, ln1_b, w_ffn_in, w_ffn_out, ln2_g, ln2_b):
    bsz, seq, _ = x.shape
    n_ctx = ctx.shape[1]
    assert seq % MIX_TILE == 0 and seq % KV_TILE == 0 and seq % FFN_TILE == 0 and seq % GRID_W == 0
    assert w_ada.shape[0] == DEPTH

    pad = (-(bsz + 1)) % 8
    c_rows = jnp.concatenate([c, c_ctx[None], jnp.zeros((pad, D_MODEL), F32)], axis=0)
    mod = _ada_rows(c_rows, w_ada[0], b_ada[0]).reshape(c_rows.shape[0], 6, D_MODEL)

    w_in_b = w_in[0].astype(BF16)
    bs_full = jnp.repeat(b_spatial[0].T, GMLP_GROUP_DIM, axis=1)
    sink_rows = jnp.repeat(attn_sink[0], BLOCK).reshape(N_KV_HEADS, GQA_GROUP * BLOCK)
    row = lambda a: a[0].reshape(1, -1)

    tables = _rope_tables(seq)
    k, vt = _kv_project(x, mod, lambda b: b, w_in_b, tables, KV_TILE)
    kc, vct = _kv_project(ctx, mod, lambda b: bsz, w_in_b, None, n_ctx)

    x_mid = _mixer(x, mod, sink_rows, tables, k, vt, kc, vct, w_in_b,
                   row(gmlp_ln_g), row(gmlp_ln_b), w_spatial[0].astype(BF16), bs_full,
                   w_branch_a[0].astype(BF16), w_branch_b[0].astype(BF16), w_out[0].astype(BF16),
                   row(ln1_g), row(ln1_b))
    return _ffn(x_mid, mod, w_ffn_in[0].astype(BF16), w_ffn_out[0].astype(BF16), row(ln2_g), row(ln2_b))
```

```python
import functools
import math

import jax
import jax.numpy as jnp
from jax import lax
from jax.experimental import pallas as pl
from jax.experimental.pallas import tpu as pltpu

F32 = jnp.float32
BF16 = jnp.bfloat16

D_MODEL = 1024
GRID_W = 64
HEAD_DIM = 64
N_Q_HEADS = 8
N_KV_HEADS = 2
GQA_GROUP = N_Q_HEADS // N_KV_HEADS
BLOCK = 128
ROPE_BASE = 10000.0
ROPE_PAIRS = HEAD_DIM // 4
N_GMLP_GROUPS = 8
GMLP_GROUP_DIM = 64
GMLP_WIDTH = N_GMLP_GROUPS * GMLP_GROUP_DIM
FFN_HIDDEN = 2816
Q_W = N_Q_HEADS * HEAD_DIM
KV_W = N_KV_HEADS * HEAD_DIM
OFF_Q, OFF_K, OFF_U = 0, Q_W, Q_W + 2 * KV_W
OFF_VB = OFF_U + GMLP_WIDTH
OFF_GA = OFF_VB + GMLP_WIDTH
OFF_GB = OFF_GA + D_MODEL
IN_W = OFF_GB + D_MODEL
LN_EPS = 1e-5
NEG_INF = -1e30
DEPTH = 1
DEEPNORM_ALPHA = (2 * DEPTH) ** 0.25
LOG2E = math.log2(math.e)
Q_SCALE = HEAD_DIM ** -0.5 * LOG2E

LANES = 128
ADA_BLOCK_N = 1536
MIX_TILE = 512
FFN_TILE = 1024
FFN_PARTS = 2
FFN_CHUNK = 256
ROW_PARTS = 2
HEAD_PIECES = 2
VMEM_LIMIT = 56 * 1024 * 1024


def _layer_norm(x):
    mu = jnp.mean(x, axis=-1, keepdims=True)
    xc = x - mu
    var = jnp.mean(xc * xc, axis=-1, keepdims=True)
    return xc * lax.rsqrt(var + LN_EPS)


def _gelu_tanh(x):
    return 0.5 * x * (1.0 + jnp.tanh(0.7978845608028654 * (x + 0.044715 * (x * x * x))))


def _rope(t, cos, sin_signed):
    lane = lax.broadcasted_iota(jnp.int32, t.shape, 1)
    partner = jnp.where((lane & ROPE_PAIRS) == 0,
                        pltpu.roll(t, LANES - ROPE_PAIRS, 1), pltpu.roll(t, ROPE_PAIRS, 1))
    return t * cos + partner * sin_signed


def _resident(shape):
    zeros = (0,) * len(shape)
    return pl.BlockSpec(shape, lambda *_: zeros, pipeline_mode=pl.Buffered(1))


def _ada_kernel(c_ref, w_ref, b_ref, o_ref):
    c = c_ref[...]
    s = (c * jax.nn.sigmoid(c)).astype(BF16)
    o_ref[...] = jnp.dot(s, w_ref[...].astype(BF16), preferred_element_type=F32) + b_ref[...]


def _ada_rows(c_rows, w_ada, b_ada):
    n = w_ada.shape[1]
    return pl.pallas_call(
        _ada_kernel,
        grid=(n // ADA_BLOCK_N,),
        in_specs=[pl.BlockSpec(c_rows.shape, lambda j: (0, 0)),
                  pl.BlockSpec((D_MODEL, ADA_BLOCK_N), lambda j: (0, j)),
                  pl.BlockSpec((1, ADA_BLOCK_N), lambda j: (0, j))],
        out_specs=pl.BlockSpec((c_rows.shape[0], ADA_BLOCK_N), lambda j: (0, j)),
        out_shape=jax.ShapeDtypeStruct((c_rows.shape[0], n), F32),
        name="ada_rows",
    )(c_rows, w_ada, b_ada.reshape(1, n))


def _ln_kv(x, m, w_kv, cos=None, sin=None):
    h = (_layer_norm(x) * (1.0 + m[1:2]) + m[0:1]).astype(BF16)
    kv = jnp.dot(h, w_kv, preferred_element_type=F32)
    k, vt = kv[:, :KV_W], kv[:, KV_W:].T
    if cos is not None:
        k = _rope(k, cos, sin)
    return h, k.astype(BF16), vt.astype(BF16)


def _kv_context_kernel(x_ref, mod_ref, w_ref, k_ref, vt_ref):
    _, k, vt = _ln_kv(x_ref[0], mod_ref[0], w_ref[...])
    for g in range(N_KV_HEADS):
        k_ref[0, g] = k[:, g * HEAD_DIM:(g + 1) * HEAD_DIM]
        vt_ref[0, g] = vt[g * HEAD_DIM:(g + 1) * HEAD_DIM, :]


def _kv_context(ctx, mod, mod_row, w_in_b):
    bsz, n_ctx, _ = ctx.shape
    return pl.pallas_call(
        _kv_context_kernel,
        grid=(bsz,),
        in_specs=[pl.BlockSpec((1, n_ctx, D_MODEL), lambda b: (b, 0, 0)),
                  pl.BlockSpec((1, 6, D_MODEL), lambda b: (mod_row, 0, 0)),
                  pl.BlockSpec((D_MODEL, 2 * KV_W), lambda b: (0, OFF_K // (2 * KV_W)))],
        out_specs=[pl.BlockSpec((1, N_KV_HEADS, n_ctx, HEAD_DIM), lambda b: (b, 0, 0, 0)),
                   pl.BlockSpec((1, N_KV_HEADS, HEAD_DIM, n_ctx), lambda b: (b, 0, 0, 0))],
        out_shape=[jax.ShapeDtypeStruct((bsz, N_KV_HEADS, n_ctx, HEAD_DIM), BF16),
                   jax.ShapeDtypeStruct((bsz, N_KV_HEADS, HEAD_DIM, n_ctx), BF16)],
        name="kv_context",
    )(ctx, mod, w_in_b)


def _attn_scores(q_blk, kk, bias_prev, bias_next, sink_row):
    n_ctx = kk.shape[0] - 3 * BLOCK
    st = lax.dot_general(kk, q_blk, (((1,), (1,)), ((), ())), preferred_element_type=F32)
    slabs = [st[:n_ctx],
             st[n_ctx:n_ctx + BLOCK] + bias_prev,
             st[n_ctx + BLOCK:n_ctx + 2 * BLOCK],
             st[n_ctx + 2 * BLOCK:] + bias_next]
    mx = sink_row
    for sl in slabs:
        mx = jnp.maximum(mx, jnp.max(sl, axis=0, keepdims=True))
    return slabs, mx


def _attn_output(slabs, mx, vvt, sink_row):
    den = jnp.exp2(sink_row - mx)
    probs = []
    for sl in slabs:
        p = jnp.exp2(sl - mx)
        den = den + jnp.sum(p, axis=0, keepdims=True)
        probs.append(p.astype(BF16))
    out_t = jnp.dot(vvt, jnp.concatenate(probs, axis=0), preferred_element_type=F32)
    return out_t / den


def _mixer_kernel(x_ref, xn_ref, mod_ref, modn_ref, cos_ref, sin_ref, cosn_ref, sinn_ref, sink_ref,
                  kc_ref, vc_ref, win_ref, glg_ref, glb_ref, ws_ref, bs_ref, wa_ref, wb_ref, wo_ref,
                  l1g_ref, l1b_ref, o_ref, h_s, k_s, vt_s, *, tile, n_tiles):
    t = pl.program_id(0)
    first_tile = lax.rem(t, n_tiles) == 0
    last_tile = lax.rem(t, n_tiles) == n_tiles - 1
    cur = lax.rem(t, 2)
    nxt = 1 - cur
    n_blk = tile // BLOCK
    w_kv = win_ref[:, OFF_K:OFF_K + 2 * KV_W]

    def store_tile(slot, h, k, vt):
        h_s[slot] = h
        for g in range(N_KV_HEADS):
            k_s[slot, g] = k[:, g * HEAD_DIM:(g + 1) * HEAD_DIM]
            vt_s[slot, g] = vt[g * HEAD_DIM:(g + 1) * HEAD_DIM, :]

    @pl.when(t == 0)
    def _():
        k_s[1] = jnp.zeros(k_s.shape[1:], BF16)
        vt_s[1] = jnp.zeros(vt_s.shape[1:], BF16)
        store_tile(0, *_ln_kv(x_ref[0], mod_ref[0], w_kv, cos_ref[...], sin_ref[...]))

    x = x_ref[0]
    m = mod_ref[0]
    h = h_s[cur]
    k_cur = [k_s[cur, g] for g in range(N_KV_HEADS)]
    vt_cur = [vt_s[cur, g] for g in range(N_KV_HEADS)]
    k_halo = [k_s[nxt, g, tile - BLOCK:, :] for g in range(N_KV_HEADS)]
    vt_halo = [vt_s[nxt, g, :, tile - BLOCK:] for g in range(N_KV_HEADS)]

    def proj(off, width):
        return jnp.dot(h, win_ref[:, off:off + width], preferred_element_type=F32)

    q = proj(OFF_Q, Q_W)
    h_next, k_next, vt_next = _ln_kv(xn_ref[0], modn_ref[0], w_kv, cosn_ref[...], sinn_ref[...])
    store_tile(nxt, h_next, k_next, vt_next)

    piece_w = 2 * LANES
    n_pieces = (IN_W - OFF_U) // piece_w
    pieces = []

    def emit_pieces(upto):
        while len(pieces) < upto:
            pieces.append(proj(OFF_U + len(pieces) * piece_w, piece_w))

    emit_pieces(HEAD_PIECES)

    cos, sin = cos_ref[...], sin_ref[...]
    q = jnp.concatenate(
        [(_rope(q[:, c * LANES:(c + 1) * LANES], cos, sin) * Q_SCALE).astype(BF16)
         for c in range(Q_W // LANES)], axis=1)

    kj = lax.broadcasted_iota(jnp.int32, (BLOCK, GQA_GROUP * BLOCK), 0)
    qi = lax.broadcasted_iota(jnp.int32, (BLOCK, GQA_GROUP * BLOCK), 1) & (BLOCK - 1)
    band_prev = jnp.where(kj >= qi, 0.0, NEG_INF)
    band_next = jnp.where(kj <= qi, 0.0, NEG_INF)
    first_prev = jnp.where(first_tile, NEG_INF, band_prev)
    last_next = jnp.where(last_tile, NEG_INF, band_next)

    k_ext = [jnp.concatenate([k_halo[g], k_cur[g], k_next[:BLOCK, g * HEAD_DIM:(g + 1) * HEAD_DIM]], axis=0)
             for g in range(N_KV_HEADS)]
    vt_ext = [jnp.concatenate([vt_halo[g], vt_cur[g], vt_next[g * HEAD_DIM:(g + 1) * HEAD_DIM, :BLOCK]],
                              axis=1) for g in range(N_KV_HEADS)]
    sink_rows = [sink_ref[g:g + 1, :] * LOG2E for g in range(N_KV_HEADS)]
    head_out = [[None] * n_blk for _ in range(N_Q_HEADS)]

    def scores(g, j):
        rows = slice(j * BLOCK, (j + 1) * BLOCK)
        q_blk = jnp.concatenate(
            [q[rows, (g * GQA_GROUP + hh) * HEAD_DIM:(g * GQA_GROUP + hh + 1) * HEAD_DIM]
             for hh in range(GQA_GROUP)], axis=0)
        kk = jnp.concatenate([kc_ref[0, g], k_ext[g][j * BLOCK:(j + 3) * BLOCK]], axis=0)
        return _attn_scores(q_blk, kk, first_prev if j == 0 else band_prev,
                            last_next if j == n_blk - 1 else band_next, sink_rows[g])

    def finish(g, j, slabs, mx):
        vvt = jnp.concatenate([vc_ref[0, g], vt_ext[g][:, j * BLOCK:(j + 3) * BLOCK]], axis=1)
        out_t = _attn_output(slabs, mx, vvt, sink_rows[g])
        for hh in range(GQA_GROUP):
            head_out[g * GQA_GROUP + hh][j] = out_t[:, hh * BLOCK:(hh + 1) * BLOCK]

    units = [(g, j) for j in range(n_blk) for g in range(N_KV_HEADS)]
    pending = None
    for u, (g, j) in enumerate(units):
        slabs, mx = scores(g, j)
        if pending is not None:
            finish(*pending)
        pending = (g, j, slabs, mx)
        emit_pieces(HEAD_PIECES + -(-(n_pieces - HEAD_PIECES) * (u + 1) // len(units)))
    finish(*pending)
    dense = jnp.concatenate(pieces, axis=1)

    def dense_cols(off, width):
        return dense[:, off - OFF_U:off - OFF_U + width]

    ya_t = jnp.concatenate([jnp.concatenate(head_out[hd], axis=1) for hd in range(N_Q_HEADS)], axis=0)
    a_proj = jnp.dot(ya_t.T.astype(BF16), wa_ref[...], preferred_element_type=F32)

    u = _gelu_tanh(dense_cols(OFF_U, GMLP_WIDTH))
    vn = (_layer_norm(_gelu_tanh(dense_cols(OFF_VB, GMLP_WIDTH))) * glg_ref[...] + glb_ref[...]).astype(BF16)
    bias = bs_ref[...]
    gated = []
    for j in range(n_blk):
        rows = slice(j * BLOCK, (j + 1) * BLOCK)
        s_chunk = jnp.concatenate(
            [jnp.dot(ws_ref[gi], vn[rows, gi * GMLP_GROUP_DIM:(gi + 1) * GMLP_GROUP_DIM],
                     preferred_element_type=F32) for gi in range(N_GMLP_GROUPS)], axis=1)
        gated.append(u[rows] * (s_chunk + bias))
    yb = jnp.concatenate(gated, axis=0)
    b_proj = jnp.dot(yb.astype(BF16), wb_ref[...], preferred_element_type=F32)

    merged = (jax.nn.sigmoid(dense_cols(OFF_GA, D_MODEL)) * a_proj
              + jax.nn.sigmoid(dense_cols(OFF_GB, D_MODEL)) * b_proj)
    merged = merged.astype(BF16)
    part = tile // ROW_PARTS
    mixes = [jnp.dot(merged[r * part:(r + 1) * part], wo_ref[...], preferred_element_type=F32)
             for r in range(ROW_PARTS)]
    for r, mix in enumerate(mixes):
        rows = slice(r * part, (r + 1) * part)
        o_ref[0, rows] = (_layer_norm(DEEPNORM_ALPHA * x[rows] + m[2:3] * mix) * l1g_ref[...]
                          + l1b_ref[...])


def _mixer(x, mod, sink_rows, tables, kc, vct, w_in_b, glg, glb, ws_b, bs_full, wa_b, wb_b, wo_b, l1g, l1b):
    bsz, seq, _ = x.shape
    tile = MIX_TILE
    n_tiles = seq // tile
    n_steps = bsz * n_tiles
    n_ctx = kc.shape[2]

    def cur(t):
        return t // n_tiles, t % n_tiles

    def nxt(t):
        return cur(jnp.minimum(t + 1, n_steps - 1))

    x_block = (1, tile, D_MODEL)
    mod_block = (1, 6, D_MODEL)
    table_block = (tile, LANES)
    in_specs = [
        pl.BlockSpec(x_block, lambda t: (*cur(t), 0)),
        pl.BlockSpec(x_block, lambda t: (*nxt(t), 0)),
        pl.BlockSpec(mod_block, lambda t: (cur(t)[0], 0, 0)),
        pl.BlockSpec(mod_block, lambda t: (nxt(t)[0], 0, 0)),
        pl.BlockSpec(table_block, lambda t: (cur(t)[1], 0)),
        pl.BlockSpec(table_block, lambda t: (cur(t)[1], 0)),
        pl.BlockSpec(table_block, lambda t: (nxt(t)[1], 0)),
        pl.BlockSpec(table_block, lambda t: (nxt(t)[1], 0)),
        _resident(sink_rows.shape),
        pl.BlockSpec((1, N_KV_HEADS, n_ctx, HEAD_DIM), lambda t: (cur(t)[0], 0, 0, 0)),
        pl.BlockSpec((1, N_KV_HEADS, HEAD_DIM, n_ctx), lambda t: (cur(t)[0], 0, 0, 0)),
        _resident(w_in_b.shape), _resident(glg.shape), _resident(glb.shape), _resident(ws_b.shape),
        _resident(bs_full.shape), _resident(wa_b.shape), _resident(wb_b.shape), _resident(wo_b.shape),
        _resident(l1g.shape), _resident(l1b.shape),
    ]
    return pl.pallas_call(
        functools.partial(_mixer_kernel, tile=tile, n_tiles=n_tiles),
        grid=(n_steps,),
        in_specs=in_specs,
        out_specs=pl.BlockSpec(x_block, lambda t: (*cur(t), 0)),
        out_shape=jax.ShapeDtypeStruct(x.shape, F32),
        scratch_shapes=[pltpu.VMEM((2, tile, D_MODEL), BF16),
                        pltpu.VMEM((2, N_KV_HEADS, tile, HEAD_DIM), BF16),
                        pltpu.VMEM((2, N_KV_HEADS, HEAD_DIM, tile), BF16)],
        compiler_params=pltpu.CompilerParams(
            dimension_semantics=("arbitrary",), vmem_limit_bytes=VMEM_LIMIT),
        name="mixer",
    )(x, x, mod, mod, tables[0], tables[1], tables[0], tables[1], sink_rows, kc, vct,
      w_in_b, glg, glb, ws_b, bs_full, wa_b, wb_b, wo_b, l1g, l1b)


def _ffn_kernel(x_ref, mod_ref, w1_ref, w2_ref, l2g_ref, l2b_ref, o_ref):
    m = mod_ref[0]
    part = x_ref.shape[1] // FFN_PARTS
    rows = [slice(r * part, (r + 1) * part) for r in range(FFN_PARTS)]
    h = [None] * FFN_PARTS
    acc = [None] * FFN_PARTS
    n_chunk = FFN_HIDDEN // FFN_CHUNK
    for c in range(n_chunk):
        for r in range(FFN_PARTS):
            if c == 0:
                h[r] = (_layer_norm(x_ref[0, rows[r]]) * (1.0 + m[4:5]) + m[3:4]).astype(BF16)
            lo = c * FFN_CHUNK
            gate = jnp.dot(h[r], w1_ref[:, lo:lo + FFN_CHUNK], preferred_element_type=F32)
            up = jnp.dot(h[r], w1_ref[:, FFN_HIDDEN + lo:FFN_HIDDEN + lo + FFN_CHUNK],
                         preferred_element_type=F32)
            act = (gate * jax.nn.sigmoid(gate) * up).astype(BF16)
            down = jnp.dot(act, w2_ref[lo:lo + FFN_CHUNK, :], preferred_element_type=F32)
            acc[r] = down if acc[r] is None else acc[r] + down
            if c == n_chunk - 1:
                o_ref[0, rows[r]] = (_layer_norm(DEEPNORM_ALPHA * x_ref[0, rows[r]] + m[5:6] * acc[r])
                                     * l2g_ref[...] + l2b_ref[...])


def _ffn(x, mod, w1_b, w2_b, l2g, l2b):
    bsz, seq, _ = x.shape
    tile = FFN_TILE
    x_spec = pl.BlockSpec((1, tile, D_MODEL), lambda b, i: (b, i, 0))
    return pl.pallas_call(
        _ffn_kernel,
        grid=(bsz, seq // tile),
        in_specs=[x_spec, pl.BlockSpec((1, 6, D_MODEL), lambda b, i: (b, 0, 0)),
                  _resident(w1_b.shape), _resident(w2_b.shape),
                  _resident(l2g.shape), _resident(l2b.shape)],
        out_specs=x_spec,
        out_shape=jax.ShapeDtypeStruct(x.shape, F32),
        compiler_params=pltpu.CompilerParams(
            dimension_semantics=("arbitrary", "arbitrary"), vmem_limit_bytes=VMEM_LIMIT),
        name="ffn",
    )(x, mod, w1_b, w2_b, l2g, l2b)


def _rope_tables(seq):
    inv = ROPE_BASE ** (-jnp.arange(ROPE_PAIRS, dtype=F32) / ROPE_PAIRS)
    t = jnp.arange(seq, dtype=jnp.int32)
    ar = (t // GRID_W).astype(F32)[:, None] * inv
    ac = (t % GRID_W).astype(F32)[:, None] * inv
    cos = jnp.concatenate([jnp.cos(ar), jnp.cos(ar), jnp.cos(ac), jnp.cos(ac)], axis=1)
    sin = jnp.concatenate([-jnp.sin(ar), jnp.sin(ar), -jnp.sin(ac), jnp.sin(ac)], axis=1)
    reps = LANES // HEAD_DIM
    return jnp.tile(cos, (1, reps)), jnp.tile(sin, (1, reps))


def kernel(x, c, ctx, c_ctx, w_ada, b_ada, w_in, attn_sink, gmlp_ln_g, gmlp_ln_b, w_spatial, b_spatial,
           w_branch_a, w_branch_b, w_out, ln1_g, ln1_b, w_ffn_in, w_ffn_out, ln2_g, ln2_b):
    bsz, seq, _ = x.shape
    assert seq % MIX_TILE == 0 and seq % FFN_TILE == 0 and seq % GRID_W == 0
    assert w_ada.shape[0] == DEPTH

    pad = (-(bsz + 1)) % 8
    c_rows = jnp.concatenate([c, c_ctx[None], jnp.zeros((pad, D_MODEL), F32)], axis=0)
    mod = _ada_rows(c_rows, w_ada[0], b_ada[0]).reshape(c_rows.shape[0], 6, D_MODEL)

    w_in_b = w_in[0].astype(BF16)
    bs_full = jnp.repeat(b_spatial[0].T, GMLP_GROUP_DIM, axis=1)
    sink_rows = jnp.repeat(attn_sink[0], BLOCK).reshape(N_KV_HEADS, GQA_GROUP * BLOCK)
    row = lambda a: a[0].reshape(1, -1)

    tables = _rope_tables(seq)
    kc, vct = _kv_context(ctx, mod, bsz, w_in_b)

    x_mid = _mixer(x, mod, sink_rows, tables, kc, vct, w_in_b,
                   row(gmlp_ln_g), row(gmlp_ln_b), w_spatial[0].astype(BF16), bs_full,
                   w_branch_a[0].astype(BF16), w_branch_b[0].astype(BF16), w_out[0].astype(BF16),
                   row(ln1_g), row(ln1_b))
    return _ffn(x_mid, mod, w_ffn_in[0].astype(BF16), w_ffn_out[0].astype(BF16), row(ln2_g), row(ln2_b))
```

```python
import functools
import math

import jax
import jax.numpy as jnp
from jax import lax
from jax.experimental import pallas as pl
from jax.experimental.pallas import tpu as pltpu

F32 = jnp.float32
BF16 = jnp.bfloat16

D_MODEL = 1024
GRID_W = 64
HEAD_DIM = 64
N_Q_HEADS = 8
N_KV_HEADS = 2
GQA_GROUP = N_Q_HEADS // N_KV_HEADS
BLOCK = 128
ROPE_BASE = 10000.0
ROPE_PAIRS = HEAD_DIM // 4
N_GMLP_GROUPS = 8
GMLP_GROUP_DIM = 64
GMLP_WIDTH = N_GMLP_GROUPS * GMLP_GROUP_DIM
FFN_HIDDEN = 2816
Q_W = N_Q_HEADS * HEAD_DIM
KV_W = N_KV_HEADS * HEAD_DIM
OFF_Q, OFF_K, OFF_U = 0, Q_W, Q_W + 2 * KV_W
OFF_VB = OFF_U + GMLP_WIDTH
OFF_GA = OFF_VB + GMLP_WIDTH
OFF_GB = OFF_GA + D_MODEL
IN_W = OFF_GB + D_MODEL
LN_EPS = 1e-5
NEG_INF = -1e30
DEPTH = 1
DEEPNORM_ALPHA = (2 * DEPTH) ** 0.25
LOG2E = math.log2(math.e)
Q_SCALE = HEAD_DIM ** -0.5 * LOG2E

LANES = 128
ADA_BLOCK_N = 1536
MIX_TILE = 512
FFN_TILE = 1024
FFN_PARTS = 2
FFN_CHUNK = 256
ROW_PARTS = 2
HEAD_PIECES = 3
LOOKAHEAD_UNIT = 2
GMLP_FIRST_UNIT = 4
VMEM_LIMIT = 56 * 1024 * 1024


def _layer_norm(x):
    mu = jnp.mean(x, axis=-1, keepdims=True)
    xc = x - mu
    var = jnp.mean(xc * xc, axis=-1, keepdims=True)
    return xc * lax.rsqrt(var + LN_EPS)


def _gelu_tanh(x):
    return 0.5 * x * (1.0 + jnp.tanh(0.7978845608028654 * (x + 0.044715 * (x * x * x))))


def _rope(t, cos, sin_signed):
    lane = lax.broadcasted_iota(jnp.int32, t.shape, 1)
    partner = jnp.where((lane & ROPE_PAIRS) == 0,
                        pltpu.roll(t, LANES - ROPE_PAIRS, 1), pltpu.roll(t, ROPE_PAIRS, 1))
    return t * cos + partner * sin_signed


def _resident(shape):
    zeros = (0,) * len(shape)
    return pl.BlockSpec(shape, lambda *_: zeros, pipeline_mode=pl.Buffered(1))


def _ada_kernel(c_ref, w_ref, b_ref, o_ref):
    c = c_ref[...]
    s = (c * jax.nn.sigmoid(c)).astype(BF16)
    o_ref[...] = jnp.dot(s, w_ref[...].astype(BF16), preferred_element_type=F32) + b_ref[...]


def _ada_rows(c_rows, w_ada, b_ada):
    n = w_ada.shape[1]
    return pl.pallas_call(
        _ada_kernel,
        grid=(n // ADA_BLOCK_N,),
        in_specs=[pl.BlockSpec(c_rows.shape, lambda j: (0, 0)),
                  pl.BlockSpec((D_MODEL, ADA_BLOCK_N), lambda j: (0, j)),
                  pl.BlockSpec((1, ADA_BLOCK_N), lambda j: (0, j))],
        out_specs=pl.BlockSpec((c_rows.shape[0], ADA_BLOCK_N), lambda j: (0, j)),
        out_shape=jax.ShapeDtypeStruct((c_rows.shape[0], n), F32),
        name="ada_rows",
    )(c_rows, w_ada, b_ada.reshape(1, n))


def _ln_kv(x, m, w_kv, cos=None, sin=None):
    h = (_layer_norm(x) * (1.0 + m[1:2]) + m[0:1]).astype(BF16)
    kv = jnp.dot(h, w_kv, preferred_element_type=F32)
    k, vt = kv[:, :KV_W], kv[:, KV_W:].T
    if cos is not None:
        k = _rope(k, cos, sin)
    return h, k.astype(BF16), vt.astype(BF16)


def _q_rope(q, cos, sin):
    return jnp.concatenate(
        [(_rope(q[:, c * LANES:(c + 1) * LANES], cos, sin) * Q_SCALE).astype(BF16)
         for c in range(Q_W // LANES)], axis=1)


def _kv_context_kernel(x_ref, mod_ref, w_ref, k_ref, vt_ref):
    _, k, vt = _ln_kv(x_ref[0], mod_ref[0], w_ref[...])
    for g in range(N_KV_HEADS):
        k_ref[0, g] = k[:, g * HEAD_DIM:(g + 1) * HEAD_DIM]
        vt_ref[0, g] = vt[g * HEAD_DIM:(g + 1) * HEAD_DIM, :]


def _kv_context(ctx, mod, mod_row, w_in_b):
    bsz, n_ctx, _ = ctx.shape
    return pl.pallas_call(
        _kv_context_kernel,
        grid=(bsz,),
        in_specs=[pl.BlockSpec((1, n_ctx, D_MODEL), lambda b: (b, 0, 0)),
                  pl.BlockSpec((1, 6, D_MODEL), lambda b: (mod_row, 0, 0)),
                  pl.BlockSpec((D_MODEL, 2 * KV_W), lambda b: (0, OFF_K // (2 * KV_W)))],
        out_specs=[pl.BlockSpec((1, N_KV_HEADS, n_ctx, HEAD_DIM), lambda b: (b, 0, 0, 0)),
                   pl.BlockSpec((1, N_KV_HEADS, HEAD_DIM, n_ctx), lambda b: (b, 0, 0, 0))],
        out_shape=[jax.ShapeDtypeStruct((bsz, N_KV_HEADS, n_ctx, HEAD_DIM), BF16),
                   jax.ShapeDtypeStruct((bsz, N_KV_HEADS, HEAD_DIM, n_ctx), BF16)],
        name="kv_context",
    )(ctx, mod, w_in_b)


def _attn_scores(q_blk, kk, bias_prev, bias_next, sink_row):
    n_ctx = kk.shape[0] - 3 * BLOCK
    st = lax.dot_general(kk, q_blk, (((1,), (1,)), ((), ())), preferred_element_type=F32)
    slabs = [st[:n_ctx],
             st[n_ctx:n_ctx + BLOCK] + bias_prev,
             st[n_ctx + BLOCK:n_ctx + 2 * BLOCK],
             st[n_ctx + 2 * BLOCK:] + bias_next]
    mx = sink_row
    for sl in slabs:
        mx = jnp.maximum(mx, jnp.max(sl, axis=0, keepdims=True))
    return slabs, mx


def _attn_output(slabs, mx, vvt, sink_row):
    den = jnp.exp2(sink_row - mx)
    probs = []
    for sl in slabs:
        p = jnp.exp2(sl - mx)
        den = den + jnp.sum(p, axis=0, keepdims=True)
        probs.append(p.astype(BF16))
    out_t = jnp.dot(vvt, jnp.concatenate(probs, axis=0), preferred_element_type=F32)
    return out_t / den


def _mixer_kernel(x_ref, xn_ref, mod_ref, modn_ref, cos_ref, sin_ref, cosn_ref, sinn_ref, sink_ref,
                  kc_ref, vc_ref, win_ref, glg_ref, glb_ref, ws_ref, bs_ref, wa_ref, wb_ref, wo_ref,
                  l1g_ref, l1b_ref, o_ref, h_s, q_s, k_s, vt_s, *, tile, n_tiles):
    t = pl.program_id(0)
    first_tile = lax.rem(t, n_tiles) == 0
    last_tile = lax.rem(t, n_tiles) == n_tiles - 1
    cur = lax.rem(t, 2)
    nxt = 1 - cur
    n_blk = tile // BLOCK
    w_kv = win_ref[:, OFF_K:OFF_K + 2 * KV_W]
    w_q = win_ref[:, OFF_Q:OFF_Q + Q_W]

    def store_tile(slot, h, k, vt):
        h_s[slot] = h
        for g in range(N_KV_HEADS):
            k_s[slot, g] = k[:, g * HEAD_DIM:(g + 1) * HEAD_DIM]
            vt_s[slot, g] = vt[g * HEAD_DIM:(g + 1) * HEAD_DIM, :]

    @pl.when(t == 0)
    def _():
        k_s[1] = jnp.zeros(k_s.shape[1:], BF16)
        vt_s[1] = jnp.zeros(vt_s.shape[1:], BF16)
        h0, k0, vt0 = _ln_kv(x_ref[0], mod_ref[0], w_kv, cos_ref[...], sin_ref[...])
        store_tile(0, h0, k0, vt0)
        q_s[0] = jnp.dot(h0, w_q, preferred_element_type=F32)

    x = x_ref[0]
    m = mod_ref[0]
    h = h_s[cur]
    q_raw = q_s[cur]
    k_cur = [k_s[cur, g] for g in range(N_KV_HEADS)]
    vt_cur = [vt_s[cur, g] for g in range(N_KV_HEADS)]
    k_halo = [k_s[nxt, g, tile - BLOCK:, :] for g in range(N_KV_HEADS)]
    vt_halo = [vt_s[nxt, g, :, tile - BLOCK:] for g in range(N_KV_HEADS)]

    def proj(off, width):
        return jnp.dot(h, win_ref[:, off:off + width], preferred_element_type=F32)

    lookahead = []

    def prepare_next():
        lookahead.extend(_ln_kv(xn_ref[0], modn_ref[0], w_kv, cosn_ref[...], sinn_ref[...]))
        store_tile(nxt, *lookahead)

    piece_w = 2 * LANES
    n_pieces = (IN_W - OFF_U) // piece_w
    pieces = []

    def emit_pieces(upto):
        while len(pieces) < upto:
            pieces.append(proj(OFF_U + len(pieces) * piece_w, piece_w))

    emit_pieces(HEAD_PIECES)

    q = _q_rope(q_raw, cos_ref[...], sin_ref[...])
    kj = lax.broadcasted_iota(jnp.int32, (BLOCK, GQA_GROUP * BLOCK), 0)
    qi = lax.broadcasted_iota(jnp.int32, (BLOCK, GQA_GROUP * BLOCK), 1) & (BLOCK - 1)
    band_prev = jnp.where(kj >= qi, 0.0, NEG_INF)
    band_next = jnp.where(kj <= qi, 0.0, NEG_INF)
    first_prev = jnp.where(first_tile, NEG_INF, band_prev)
    last_next = jnp.where(last_tile, NEG_INF, band_next)

    def k_block(g, b):
        if b < 0:
            return k_halo[g]
        if b == n_blk:
            return lookahead[1][:BLOCK, g * HEAD_DIM:(g + 1) * HEAD_DIM]
        return k_cur[g][b * BLOCK:(b + 1) * BLOCK]

    def vt_block(g, b):
        if b < 0:
            return vt_halo[g]
        if b == n_blk:
            return lookahead[2][g * HEAD_DIM:(g + 1) * HEAD_DIM, :BLOCK]
        return vt_cur[g][:, b * BLOCK:(b + 1) * BLOCK]

    sink_rows = [sink_ref[g:g + 1, :] * LOG2E for g in range(N_KV_HEADS)]
    head_out = [[None] * n_blk for _ in range(N_Q_HEADS)]

    def scores(g, j):
        rows = slice(j * BLOCK, (j + 1) * BLOCK)
        q_blk = jnp.concatenate(
            [q[rows, (g * GQA_GROUP + hh) * HEAD_DIM:(g * GQA_GROUP + hh + 1) * HEAD_DIM]
             for hh in range(GQA_GROUP)], axis=0)
        kk = jnp.concatenate([kc_ref[0, g]] + [k_block(g, b) for b in (j - 1, j, j + 1)], axis=0)
        return _attn_scores(q_blk, kk, first_prev if j == 0 else band_prev,
                            last_next if j == n_blk - 1 else band_next, sink_rows[g])

    def finish(g, j, slabs, mx):
        vvt = jnp.concatenate([vc_ref[0, g]] + [vt_block(g, b) for b in (j - 1, j, j + 1)], axis=1)
        out_t = _attn_output(slabs, mx, vvt, sink_rows[g])
        for hh in range(GQA_GROUP):
            head_out[g * GQA_GROUP + hh][j] = out_t[:, hh * BLOCK:(hh + 1) * BLOCK]

    def piece_cols(off, width):
        first = (off - OFF_U) // piece_w
        return jnp.concatenate(pieces[first:first + width // piece_w], axis=1)

    gmlp = {"gated": [None] * n_blk}

    def gmlp_u():
        gmlp["u"] = _gelu_tanh(piece_cols(OFF_U, GMLP_WIDTH))

    def gmlp_vn():
        vb = _gelu_tanh(piece_cols(OFF_VB, GMLP_WIDTH))
        gmlp["vn"] = (_layer_norm(vb) * glg_ref[...] + glb_ref[...]).astype(BF16)

    def gmlp_block(j):
        rows = slice(j * BLOCK, (j + 1) * BLOCK)
        s_chunk = jnp.concatenate(
            [jnp.dot(ws_ref[gi], gmlp["vn"][rows, gi * GMLP_GROUP_DIM:(gi + 1) * GMLP_GROUP_DIM],
                     preferred_element_type=F32) for gi in range(N_GMLP_GROUPS)], axis=1)
        gmlp["gated"][j] = gmlp["u"][rows] * (s_chunk + bs_ref[...])

    side_tasks = {1: gmlp_u, 3: gmlp_vn}
    side_tasks.update({GMLP_FIRST_UNIT + j: functools.partial(gmlp_block, j) for j in range(n_blk)})

    units = [(g, j) for j in range(n_blk) for g in range(N_KV_HEADS)]
    pending = None
    for u, (g, j) in enumerate(units):
        if u == LOOKAHEAD_UNIT:
            prepare_next()
        slabs, mx = scores(g, j)
        if pending is not None:
            finish(*pending)
        pending = (g, j, slabs, mx)
        emit_pieces(HEAD_PIECES + -(-(n_pieces - HEAD_PIECES) * (u + 1) // len(units)))
        if u in side_tasks:
            side_tasks[u]()
    finish(*pending)

    ya_t = jnp.concatenate([jnp.concatenate(head_out[hd], axis=1) for hd in range(N_Q_HEADS)], axis=0)
    a_proj = jnp.dot(ya_t.T.astype(BF16), wa_ref[...], preferred_element_type=F32)
    yb = jnp.concatenate(gmlp["gated"], axis=0)
    b_proj = jnp.dot(yb.astype(BF16), wb_ref[...], preferred_element_type=F32)

    merged = (jax.nn.sigmoid(piece_cols(OFF_GA, D_MODEL)) * a_proj
              + jax.nn.sigmoid(piece_cols(OFF_GB, D_MODEL)) * b_proj)
    merged = merged.astype(BF16)
    part = tile // ROW_PARTS
    mixes = [jnp.dot(merged[r * part:(r + 1) * part], wo_ref[...], preferred_element_type=F32)
             for r in range(ROW_PARTS)]
    q_s[nxt] = jnp.dot(lookahead[0], w_q, preferred_element_type=F32)
    for r, mix in enumerate(mixes):
        rows = slice(r * part, (r + 1) * part)
        o_ref[0, rows] = (_layer_norm(DEEPNORM_ALPHA * x[rows] + m[2:3] * mix) * l1g_ref[...]
                          + l1b_ref[...])


def _mixer(x, mod, sink_rows, tables, kc, vct, w_in_b, glg, glb, ws_b, bs_full, wa_b, wb_b, wo_b, l1g, l1b):
    bsz, seq, _ = x.shape
    tile = MIX_TILE
    n_tiles = seq // tile
    n_steps = bsz * n_tiles
    n_ctx = kc.shape[2]

    def cur(t):
        return t // n_tiles, t % n_tiles

    def nxt(t):
        return cur(jnp.minimum(t + 1, n_steps - 1))

    x_block = (1, tile, D_MODEL)
    mod_block = (1, 6, D_MODEL)
    table_block = (tile, LANES)
    in_specs = [
        pl.BlockSpec(x_block, lambda t: (*cur(t), 0)),
        pl.BlockSpec(x_block, lambda t: (*nxt(t), 0)),
        pl.BlockSpec(mod_block, lambda t: (cur(t)[0], 0, 0)),
        pl.BlockSpec(mod_block, lambda t: (nxt(t)[0], 0, 0)),
        pl.BlockSpec(table_block, lambda t: (cur(t)[1], 0)),
        pl.BlockSpec(table_block, lambda t: (cur(t)[1], 0)),
        pl.BlockSpec(table_block, lambda t: (nxt(t)[1], 0)),
        pl.BlockSpec(table_block, lambda t: (nxt(t)[1], 0)),
        _resident(sink_rows.shape),
        pl.BlockSpec((1, N_KV_HEADS, n_ctx, HEAD_DIM), lambda t: (cur(t)[0], 0, 0, 0)),
        pl.BlockSpec((1, N_KV_HEADS, HEAD_DIM, n_ctx), lambda t: (cur(t)[0], 0, 0, 0)),
        _resident(w_in_b.shape), _resident(glg.shape), _resident(glb.shape), _resident(ws_b.shape),
        _resident(bs_full.shape), _resident(wa_b.shape), _resident(wb_b.shape), _resident(wo_b.shape),
        _resident(l1g.shape), _resident(l1b.shape),
    ]
    return pl.pallas_call(
        functools.partial(_mixer_kernel, tile=tile, n_tiles=n_tiles),
        grid=(n_steps,),
        in_specs=in_specs,
        out_specs=pl.BlockSpec(x_block, lambda t: (*cur(t), 0)),
        out_shape=jax.ShapeDtypeStruct(x.shape, F32),
        scratch_shapes=[pltpu.VMEM((2, tile, D_MODEL), BF16),
                        pltpu.VMEM((2, tile, Q_W), F32),
                        pltpu.VMEM((2, N_KV_HEADS, tile, HEAD_DIM), BF16),
                        pltpu.VMEM((2, N_KV_HEADS, HEAD_DIM, tile), BF16)],
        compiler_params=pltpu.CompilerParams(
            dimension_semantics=("arbitrary",), vmem_limit_bytes=VMEM_LIMIT),
        name="mixer",
    )(x, x, mod, mod, tables[0], tables[1], tables[0], tables[1], sink_rows, kc, vct,
      w_in_b, glg, glb, ws_b, bs_full, wa_b, wb_b, wo_b, l1g, l1b)


def _ffn_kernel(x_ref, mod_ref, wg_ref, wu_ref, wd_ref, l2g_ref, l2b_ref, o_ref, wg_s, wu_s, wd_s):
    s = pl.program_id(0)
    n_chunk = wg_s.shape[0]

    @pl.when(s < n_chunk)
    def _():
        wg_s[s] = wg_ref[...].astype(BF16)
        wu_s[s] = wu_ref[...].astype(BF16)
        wd_s[s] = wd_ref[...].astype(BF16)

    @pl.when(s >= n_chunk)
    def _():
        m = mod_ref[0]
        part = x_ref.shape[1] // FFN_PARTS
        rows = [slice(r * part, (r + 1) * part) for r in range(FFN_PARTS)]
        h = [None] * FFN_PARTS
        acc = [None] * FFN_PARTS
        for c in range(n_chunk):
            for r in range(FFN_PARTS):
                if c == 0:
                    h[r] = (_layer_norm(x_ref[0, rows[r]]) * (1.0 + m[4:5]) + m[3:4]).astype(BF16)
                gate = jnp.dot(h[r], wg_s[c], preferred_element_type=F32)
                up = jnp.dot(h[r], wu_s[c], preferred_element_type=F32)
                act = (gate * jax.nn.sigmoid(gate) * up).astype(BF16)
                down = jnp.dot(act, wd_s[c], preferred_element_type=F32)
                acc[r] = down if acc[r] is None else acc[r] + down
                if c == n_chunk - 1:
                    o_ref[0, rows[r]] = (_layer_norm(DEEPNORM_ALPHA * x_ref[0, rows[r]] + m[5:6] * acc[r])
                                         * l2g_ref[...] + l2b_ref[...])


def _ffn(x, mod, w_in_f32, w_out_f32, l2g, l2b):
    bsz, seq, _ = x.shape
    tile = FFN_TILE
    n_tiles = seq // tile
    n_chunk = FFN_HIDDEN // FFN_CHUNK

    def tile_at(s):
        t = jnp.maximum(s - n_chunk, 0)
        return t // n_tiles, t % n_tiles

    def chunk_at(s):
        return jnp.minimum(s, n_chunk - 1)

    x_spec = pl.BlockSpec((1, tile, D_MODEL), lambda s: (*tile_at(s), 0))
    return pl.pallas_call(
        _ffn_kernel,
        grid=(n_chunk + bsz * n_tiles,),
        in_specs=[x_spec, pl.BlockSpec((1, 6, D_MODEL), lambda s: (tile_at(s)[0], 0, 0)),
                  pl.BlockSpec((D_MODEL, FFN_CHUNK), lambda s: (0, chunk_at(s))),
                  pl.BlockSpec((D_MODEL, FFN_CHUNK), lambda s: (0, n_chunk + chunk_at(s))),
                  pl.BlockSpec((FFN_CHUNK, D_MODEL), lambda s: (chunk_at(s), 0)),
                  _resident(l2g.shape), _resident(l2b.shape)],
        out_specs=x_spec,
        out_shape=jax.ShapeDtypeStruct(x.shape, F32),
        scratch_shapes=[pltpu.VMEM((n_chunk, D_MODEL, FFN_CHUNK), BF16),
                        pltpu.VMEM((n_chunk, D_MODEL, FFN_CHUNK), BF16),
                        pltpu.VMEM((n_chunk, FFN_CHUNK, D_MODEL), BF16)],
        compiler_params=pltpu.CompilerParams(
            dimension_semantics=("arbitrary",), vmem_limit_bytes=VMEM_LIMIT),
        name="ffn",
    )(x, mod, w_in_f32, w_in_f32, w_out_f32, l2g, l2b)


def _rope_tables(seq):
    inv = ROPE_BASE ** (-jnp.arange(ROPE_PAIRS, dtype=F32) / ROPE_PAIRS)
    t = jnp.arange(seq, dtype=jnp.int32)
    ar = (t // GRID_W).astype(F32)[:, None] * inv
    ac = (t % GRID_W).astype(F32)[:, None] * inv
    cos = jnp.concatenate([jnp.cos(ar), jnp.cos(ar), jnp.cos(ac), jnp.cos(ac)], axis=1)
    sin = jnp.concatenate([-jnp.sin(ar), jnp.sin(ar), -jnp.sin(ac), jnp.sin(ac)], axis=1)
    reps = LANES // HEAD_DIM
    return jnp.tile(cos, (1, reps)), jnp.tile(sin, (1, reps))


def kernel(x, c, ctx, c_ctx, w_ada, b_ada, w_in, attn_sink, gmlp_ln_g, gmlp_ln_b, w_spatial, b_spatial,
           w_branch_a, w_branch_b, w_out, ln1_g, ln1_b, w_ffn_in, w_ffn_out, ln2_g, ln2_b):
    bsz, seq, _ = x.shape
    assert seq % MIX_TILE == 0 and seq % FFN_TILE == 0 and seq % GRID_W == 0
    assert w_ada.shape[0] == DEPTH

    pad = (-(bsz + 1)) % 8
    c_rows = jnp.concatenate([c, c_ctx[None], jnp.zeros((pad, D_MODEL), F32)], axis=0)
    mod = _ada_rows(c_rows, w_ada[0], b_ada[0]).reshape(c_rows.shape[0], 6, D_MODEL)

    w_in_b = w_in[0].astype(BF16)
    bs_full = jnp.repeat(b_spatial[0].T, GMLP_GROUP_DIM, axis=1)
    sink_rows = jnp.repeat(attn_sink[0], BLOCK).reshape(N_KV_HEADS, GQA_GROUP * BLOCK)
    row = lambda a: a[0].reshape(1, -1)

    tables = _rope_tables(seq)
    kc, vct = _kv_context(ctx, mod, bsz, w_in_b)

    x_mid = _mixer(x, mod, sink_rows, tables, kc, vct, w_in_b,
                   row(gmlp_ln_g), row(gmlp_ln_b), w_spatial[0].astype(BF16), bs_full,
                   w_branch_a[0].astype(BF16), w_branch_b[0].astype(BF16), w_out[0].astype(BF16),
                   row(ln1_g), row(ln1_b))
    return _ffn(x_mid, mod, w_ffn_in[0], w_ffn_out[0], row(ln2_g), row(ln2_b))
```

```python
import functools
import math

import jax
import jax.numpy as jnp
from jax import lax
from jax.experimental import pallas as pl
from jax.experimental.pallas import tpu as pltpu

F32 = jnp.float32
BF16 = jnp.bfloat16

D_MODEL = 1024
GRID_W = 64
HEAD_DIM = 64
N_Q_HEADS = 8
N_KV_HEADS = 2
GQA_GROUP = N_Q_HEADS // N_KV_HEADS
BLOCK = 128
ROPE_BASE = 10000.0
ROPE_PAIRS = HEAD_DIM // 4
N_GMLP_GROUPS = 8
GMLP_GROUP_DIM = 64
GMLP_WIDTH = N_GMLP_GROUPS * GMLP_GROUP_DIM
FFN_HIDDEN = 2816
Q_W = N_Q_HEADS * HEAD_DIM
KV_W = N_KV_HEADS * HEAD_DIM
OFF_Q, OFF_K, OFF_U = 0, Q_W, Q_W + 2 * KV_W
OFF_VB = OFF_U + GMLP_WIDTH
OFF_GA = OFF_VB + GMLP_WIDTH
OFF_GB = OFF_GA + D_MODEL
IN_W = OFF_GB + D_MODEL
LN_EPS = 1e-5
NEG_INF = -1e30
DEPTH = 1
DEEPNORM_ALPHA = (2 * DEPTH) ** 0.25
LOG2E = math.log2(math.e)
Q_SCALE = HEAD_DIM ** -0.5 * LOG2E

LANES = 128
ADA_BLOCK_N = 1536
MIX_TILE = 512
FFN_TILE = 1024
FFN_PARTS = 2
FFN_CHUNK = 256
ROW_PARTS = 2
HEAD_PIECES = 3
LOOKAHEAD_UNIT = 2
GMLP_FIRST_UNIT = 4
VMEM_LIMIT = 56 * 1024 * 1024


def _layer_norm(x, eps=LN_EPS):
    mu = jnp.mean(x, axis=-1, keepdims=True)
    xc = x - mu
    var = jnp.mean(xc * xc, axis=-1, keepdims=True)
    return xc * lax.rsqrt(var + eps)


def _post_norm(res, gate, out, g, b):
    z = res + (gate * (1.0 / DEEPNORM_ALPHA)) * out
    return _layer_norm(z, LN_EPS / DEEPNORM_ALPHA ** 2) * g + b


def _gelu_tanh(x):
    c = 0.7978845608028654
    return 0.5 * x * (1.0 + jnp.tanh(x * (c + (0.044715 * c) * (x * x))))


def _rope(t, cos, sin_signed):
    lane = lax.broadcasted_iota(jnp.int32, t.shape, 1)
    partner = jnp.where((lane & ROPE_PAIRS) == 0,
                        pltpu.roll(t, LANES - ROPE_PAIRS, 1), pltpu.roll(t, ROPE_PAIRS, 1))
    return t * cos + partner * sin_signed


def _resident(shape):
    zeros = (0,) * len(shape)
    return pl.BlockSpec(shape, lambda *_: zeros, pipeline_mode=pl.Buffered(1))


def _ada_kernel(c_ref, w_ref, b_ref, o_ref):
    c = c_ref[...]
    s = (c * jax.nn.sigmoid(c)).astype(BF16)
    o_ref[...] = jnp.dot(s, w_ref[...].astype(BF16), preferred_element_type=F32) + b_ref[...]


def _ada_rows(c_rows, w_ada, b_ada):
    n = w_ada.shape[1]
    return pl.pallas_call(
        _ada_kernel,
        grid=(n // ADA_BLOCK_N,),
        in_specs=[pl.BlockSpec(c_rows.shape, lambda j: (0, 0)),
                  pl.BlockSpec((D_MODEL, ADA_BLOCK_N), lambda j: (0, j)),
                  pl.BlockSpec((1, ADA_BLOCK_N), lambda j: (0, j))],
        out_specs=pl.BlockSpec((c_rows.shape[0], ADA_BLOCK_N), lambda j: (0, j)),
        out_shape=jax.ShapeDtypeStruct((c_rows.shape[0], n), F32),
        name="ada_rows",
    )(c_rows, w_ada, b_ada.reshape(1, n))


def _ln_kv(x, m, w_kv, cos=None, sin=None):
    h = (_layer_norm(x) * (1.0 + m[1:2]) + m[0:1]).astype(BF16)
    kv = jnp.dot(h, w_kv, preferred_element_type=F32)
    k, vt = kv[:, :KV_W], kv[:, KV_W:].T
    if cos is not None:
        k = _rope(k, cos, sin)
    return h, k.astype(BF16), vt.astype(BF16)


def _q_rope(q, cos, sin):
    return jnp.concatenate(
        [(_rope(q[:, c * LANES:(c + 1) * LANES], cos, sin) * Q_SCALE).astype(BF16)
         for c in range(Q_W // LANES)], axis=1)


def _kv_context_kernel(x_ref, mod_ref, w_ref, k_ref, vt_ref):
    _, k, vt = _ln_kv(x_ref[0], mod_ref[0], w_ref[...])
    for g in range(N_KV_HEADS):
        k_ref[0, g] = k[:, g * HEAD_DIM:(g + 1) * HEAD_DIM]
        vt_ref[0, g] = vt[g * HEAD_DIM:(g + 1) * HEAD_DIM, :]


def _kv_context(ctx, mod, mod_row, w_in_b):
    bsz, n_ctx, _ = ctx.shape
    return pl.pallas_call(
        _kv_context_kernel,
        grid=(bsz,),
        in_specs=[pl.BlockSpec((1, n_ctx, D_MODEL), lambda b: (b, 0, 0)),
                  pl.BlockSpec((1, 6, D_MODEL), lambda b: (mod_row, 0, 0)),
                  pl.BlockSpec((D_MODEL, 2 * KV_W), lambda b: (0, OFF_K // (2 * KV_W)))],
        out_specs=[pl.BlockSpec((1, N_KV_HEADS, n_ctx, HEAD_DIM), lambda b: (b, 0, 0, 0)),
                   pl.BlockSpec((1, N_KV_HEADS, HEAD_DIM, n_ctx), lambda b: (b, 0, 0, 0))],
        out_shape=[jax.ShapeDtypeStruct((bsz, N_KV_HEADS, n_ctx, HEAD_DIM), BF16),
                   jax.ShapeDtypeStruct((bsz, N_KV_HEADS, HEAD_DIM, n_ctx), BF16)],
        name="kv_context",
    )(ctx, mod, w_in_b)


def _attn_scores(q_blk, kk, bias_prev, bias_next, sink_row):
    n_ctx = kk.shape[0] - 3 * BLOCK
    st = lax.dot_general(kk, q_blk, (((1,), (1,)), ((), ())), preferred_element_type=F32)
    slabs = [st[:n_ctx],
             st[n_ctx:n_ctx + BLOCK] + bias_prev,
             st[n_ctx + BLOCK:n_ctx + 2 * BLOCK],
             st[n_ctx + 2 * BLOCK:] + bias_next]
    mx = sink_row
    for sl in slabs:
        mx = jnp.maximum(mx, jnp.max(sl, axis=0, keepdims=True))
    return slabs, mx


def _attn_output(slabs, mx, vvt, sink_row):
    den = jnp.exp2(sink_row - mx)
    probs = []
    for sl in slabs:
        p = jnp.exp2(sl - mx)
        den = den + jnp.sum(p, axis=0, keepdims=True)
        probs.append(p.astype(BF16))
    out_t = jnp.dot(vvt, jnp.concatenate(probs, axis=0), preferred_element_type=F32)
    return out_t / den


def _mixer_kernel(x_ref, xn_ref, mod_ref, modn_ref, cos_ref, sin_ref, cosn_ref, sinn_ref, sink_ref,
                  kc_ref, vc_ref, win_ref, glg_ref, glb_ref, ws_ref, bs_ref, wa_ref, wb_ref, wo_ref,
                  l1g_ref, l1b_ref, fg_ref, fu_ref, fd_ref, o_ref, fg_o, fu_o, fd_o,
                  h_s, q_s, k_s, vt_s, *, tile, n_tiles):
    t = pl.program_id(0)

    @pl.when(t < FFN_HIDDEN // FFN_CHUNK)
    def _():
        fg_o[0] = fg_ref[...].astype(BF16)
        fu_o[0] = fu_ref[...].astype(BF16)
        fd_o[0] = fd_ref[...].astype(BF16)

    first_tile = lax.rem(t, n_tiles) == 0
    last_tile = lax.rem(t, n_tiles) == n_tiles - 1
    cur = lax.rem(t, 2)
    nxt = 1 - cur
    n_blk = tile // BLOCK
    w_kv = win_ref[:, OFF_K:OFF_K + 2 * KV_W]
    w_q = win_ref[:, OFF_Q:OFF_Q + Q_W]

    def store_tile(slot, h, k, vt):
        h_s[slot] = h
        for g in range(N_KV_HEADS):
            k_s[slot, g] = k[:, g * HEAD_DIM:(g + 1) * HEAD_DIM]
            vt_s[slot, g] = vt[g * HEAD_DIM:(g + 1) * HEAD_DIM, :]

    @pl.when(t == 0)
    def _():
        k_s[1] = jnp.zeros(k_s.shape[1:], BF16)
        vt_s[1] = jnp.zeros(vt_s.shape[1:], BF16)
        h0, k0, vt0 = _ln_kv(x_ref[0], mod_ref[0], w_kv, cos_ref[...], sin_ref[...])
        store_tile(0, h0, k0, vt0)
        q_s[0] = jnp.dot(h0, w_q, preferred_element_type=F32)

    x = x_ref[0]
    m = mod_ref[0]
    h = h_s[cur]
    q_raw = q_s[cur]
    k_cur = [k_s[cur, g] for g in range(N_KV_HEADS)]
    vt_cur = [vt_s[cur, g] for g in range(N_KV_HEADS)]
    k_halo = [k_s[nxt, g, tile - BLOCK:, :] for g in range(N_KV_HEADS)]
    vt_halo = [vt_s[nxt, g, :, tile - BLOCK:] for g in range(N_KV_HEADS)]

    def proj(off, width):
        return jnp.dot(h, win_ref[:, off:off + width], preferred_element_type=F32)

    lookahead = []

    def prepare_next():
        lookahead.extend(_ln_kv(xn_ref[0], modn_ref[0], w_kv, cosn_ref[...], sinn_ref[...]))
        store_tile(nxt, *lookahead)

    piece_w = 2 * LANES
    n_pieces = (IN_W - OFF_U) // piece_w
    pieces = []

    def emit_pieces(upto):
        while len(pieces) < upto:
            pieces.append(proj(OFF_U + len(pieces) * piece_w, piece_w))

    emit_pieces(HEAD_PIECES)

    q = _q_rope(q_raw, cos_ref[...], sin_ref[...])
    kj = lax.broadcasted_iota(jnp.int32, (BLOCK, GQA_GROUP * BLOCK), 0)
    qi = lax.broadcasted_iota(jnp.int32, (BLOCK, GQA_GROUP * BLOCK), 1) & (BLOCK - 1)
    band_prev = jnp.where(kj >= qi, 0.0, NEG_INF)
    band_next = jnp.where(kj <= qi, 0.0, NEG_INF)
    first_prev = jnp.where(first_tile, NEG_INF, band_prev)
    last_next = jnp.where(last_tile, NEG_INF, band_next)

    def k_block(g, b):
        if b < 0:
            return k_halo[g]
        if b == n_blk:
            return lookahead[1][:BLOCK, g * HEAD_DIM:(g + 1) * HEAD_DIM]
        return k_cur[g][b * BLOCK:(b + 1) * BLOCK]

    def vt_block(g, b):
        if b < 0:
            return vt_halo[g]
        if b == n_blk:
            return lookahead[2][g * HEAD_DIM:(g + 1) * HEAD_DIM, :BLOCK]
        return vt_cur[g][:, b * BLOCK:(b + 1) * BLOCK]

    sink_rows = [sink_ref[g:g + 1, :] * LOG2E for g in range(N_KV_HEADS)]
    head_out = [[None] * n_blk for _ in range(N_Q_HEADS)]

    def scores(g, j):
        rows = slice(j * BLOCK, (j + 1) * BLOCK)
        q_blk = jnp.concatenate(
            [q[rows, (g * GQA_GROUP + hh) * HEAD_DIM:(g * GQA_GROUP + hh + 1) * HEAD_DIM]
             for hh in range(GQA_GROUP)], axis=0)
        kk = jnp.concatenate([kc_ref[0, g]] + [k_block(g, b) for b in (j - 1, j, j + 1)], axis=0)
        return _attn_scores(q_blk, kk, first_prev if j == 0 else band_prev,
                            last_next if j == n_blk - 1 else band_next, sink_rows[g])

    def finish(g, j, slabs, mx):
        vvt = jnp.concatenate([vc_ref[0, g]] + [vt_block(g, b) for b in (j - 1, j, j + 1)], axis=1)
        out_t = _attn_output(slabs, mx, vvt, sink_rows[g])
        for hh in range(GQA_GROUP):
            head_out[g * GQA_GROUP + hh][j] = out_t[:, hh * BLOCK:(hh + 1) * BLOCK]

    def piece_cols(off, width):
        first = (off - OFF_U) // piece_w
        return jnp.concatenate(pieces[first:first + width // piece_w], axis=1)

    gmlp = {"gated": [None] * n_blk}

    def gmlp_u():
        gmlp["u"] = _gelu_tanh(piece_cols(OFF_U, GMLP_WIDTH))

    def gmlp_vn():
        vb = _gelu_tanh(piece_cols(OFF_VB, GMLP_WIDTH))
        gmlp["vn"] = (_layer_norm(vb) * glg_ref[...] + glb_ref[...]).astype(BF16)

    def gmlp_block(j):
        rows = slice(j * BLOCK, (j + 1) * BLOCK)
        left = lax.broadcasted_iota(jnp.int32, (BLOCK, LANES), 1) < GMLP_GROUP_DIM
        cols = []
        for p in range(N_GMLP_GROUPS // 2):
            pair = gmlp["vn"][rows, p * LANES:(p + 1) * LANES]
            zero = jnp.zeros_like(pair)
            rhs = jnp.concatenate([jnp.where(left, pair, zero), jnp.where(left, zero, pair)], axis=0)
            cols.append(jnp.dot(ws_ref[p], rhs, preferred_element_type=F32))
        s_chunk = jnp.concatenate(cols, axis=1)
        gmlp["gated"][j] = gmlp["u"][rows] * (s_chunk + bs_ref[...])

    side_tasks = {1: gmlp_u, 3: gmlp_vn}
    side_tasks.update({GMLP_FIRST_UNIT + j: functools.partial(gmlp_block, j) for j in range(n_blk)})

    units = [(g, j) for j in range(n_blk) for g in range(N_KV_HEADS)]
    pending = None
    for u, (g, j) in enumerate(units):
        if u == LOOKAHEAD_UNIT:
            prepare_next()
        slabs, mx = scores(g, j)
        if pending is not None:
            finish(*pending)
        pending = (g, j, slabs, mx)
        emit_pieces(HEAD_PIECES + -(-(n_pieces - HEAD_PIECES) * (u + 1) // len(units)))
        if u in side_tasks:
            side_tasks[u]()
    finish(*pending)

    ya_t = jnp.concatenate([jnp.concatenate(head_out[hd], axis=1) for hd in range(N_Q_HEADS)], axis=0)
    a_proj = jnp.dot(ya_t.T.astype(BF16), wa_ref[...], preferred_element_type=F32)
    yb = jnp.concatenate(gmlp["gated"], axis=0)
    b_proj = jnp.dot(yb.astype(BF16), wb_ref[...], preferred_element_type=F32)

    merged = (jax.nn.sigmoid(piece_cols(OFF_GA, D_MODEL)) * a_proj
              + jax.nn.sigmoid(piece_cols(OFF_GB, D_MODEL)) * b_proj)
    merged = merged.astype(BF16)
    part = tile // ROW_PARTS
    mixes = [jnp.dot(merged[r * part:(r + 1) * part], wo_ref[...], preferred_element_type=F32)
             for r in range(ROW_PARTS)]
    q_s[nxt] = jnp.dot(lookahead[0], w_q, preferred_element_type=F32)
    for r, mix in enumerate(mixes):
        rows = slice(r * part, (r + 1) * part)
        o_ref[0, rows] = _post_norm(x[rows], m[2:3], mix, l1g_ref[...], l1b_ref[...])


def _mixer(x, mod, sink_rows, tables, kc, vct, w_in_b, glg, glb, ws_b, bs_full, wa_b, wb_b, wo_b, l1g, l1b,
           w_ffn_in, w_ffn_out):
    bsz, seq, _ = x.shape
    tile = MIX_TILE
    n_tiles = seq // tile
    n_steps = bsz * n_tiles
    n_ctx = kc.shape[2]

    def cur(t):
        return t // n_tiles, t % n_tiles

    def nxt(t):
        return cur(jnp.minimum(t + 1, n_steps - 1))

    n_chunk = FFN_HIDDEN // FFN_CHUNK
    assert n_chunk <= n_steps

    def chunk(t):
        return jnp.minimum(t, n_chunk - 1)

    x_block = (1, tile, D_MODEL)
    mod_block = (1, 6, D_MODEL)
    table_block = (tile, LANES)
    in_specs = [
        pl.BlockSpec(x_block, lambda t: (*cur(t), 0)),
        pl.BlockSpec(x_block, lambda t: (*nxt(t), 0)),
        pl.BlockSpec(mod_block, lambda t: (cur(t)[0], 0, 0)),
        pl.BlockSpec(mod_block, lambda t: (nxt(t)[0], 0, 0)),
        pl.BlockSpec(table_block, lambda t: (cur(t)[1], 0)),
        pl.BlockSpec(table_block, lambda t: (cur(t)[1], 0)),
        pl.BlockSpec(table_block, lambda t: (nxt(t)[1], 0)),
        pl.BlockSpec(table_block, lambda t: (nxt(t)[1], 0)),
        _resident(sink_rows.shape),
        pl.BlockSpec((1, N_KV_HEADS, n_ctx, HEAD_DIM), lambda t: (cur(t)[0], 0, 0, 0)),
        pl.BlockSpec((1, N_KV_HEADS, HEAD_DIM, n_ctx), lambda t: (cur(t)[0], 0, 0, 0)),
        _resident(w_in_b.shape), _resident(glg.shape), _resident(glb.shape), _resident(ws_b.shape),
        _resident(bs_full.shape), _resident(wa_b.shape), _resident(wb_b.shape), _resident(wo_b.shape),
        _resident(l1g.shape), _resident(l1b.shape),
        pl.BlockSpec((D_MODEL, FFN_CHUNK), lambda t: (0, chunk(t))),
        pl.BlockSpec((D_MODEL, FFN_CHUNK), lambda t: (0, n_chunk + chunk(t))),
        pl.BlockSpec((FFN_CHUNK, D_MODEL), lambda t: (chunk(t), 0)),
    ]
    return pl.pallas_call(
        functools.partial(_mixer_kernel, tile=tile, n_tiles=n_tiles),
        grid=(n_steps,),
        in_specs=in_specs,
        out_specs=[pl.BlockSpec(x_block, lambda t: (*cur(t), 0)),
                   pl.BlockSpec((1, D_MODEL, FFN_CHUNK), lambda t: (chunk(t), 0, 0)),
                   pl.BlockSpec((1, D_MODEL, FFN_CHUNK), lambda t: (chunk(t), 0, 0)),
                   pl.BlockSpec((1, FFN_CHUNK, D_MODEL), lambda t: (chunk(t), 0, 0))],
        out_shape=[jax.ShapeDtypeStruct(x.shape, F32),
                   jax.ShapeDtypeStruct((n_chunk, D_MODEL, FFN_CHUNK), BF16),
                   jax.ShapeDtypeStruct((n_chunk, D_MODEL, FFN_CHUNK), BF16),
                   jax.ShapeDtypeStruct((n_chunk, FFN_CHUNK, D_MODEL), BF16)],
        scratch_shapes=[pltpu.VMEM((2, tile, D_MODEL), BF16),
                        pltpu.VMEM((2, tile, Q_W), F32),
                        pltpu.VMEM((2, N_KV_HEADS, tile, HEAD_DIM), BF16),
                        pltpu.VMEM((2, N_KV_HEADS, HEAD_DIM, tile), BF16)],
        compiler_params=pltpu.CompilerParams(
            dimension_semantics=("arbitrary",), vmem_limit_bytes=VMEM_LIMIT),
        name="mixer",
    )(x, x, mod, mod, tables[0], tables[1], tables[0], tables[1], sink_rows, kc, vct,
      w_in_b, glg, glb, ws_b, bs_full, wa_b, wb_b, wo_b, l1g, l1b, w_ffn_in, w_ffn_in, w_ffn_out)


def _ffn_kernel(x_ref, mod_ref, wg_ref, wu_ref, wd_ref, l2g_ref, l2b_ref, o_ref):
    m = mod_ref[0]
    part = x_ref.shape[1] // FFN_PARTS
    rows = [slice(r * part, (r + 1) * part) for r in range(FFN_PARTS)]
    h = [None] * FFN_PARTS
    acc = [None] * FFN_PARTS
    n_chunk = wg_ref.shape[0]
    for c in range(n_chunk):
        for r in range(FFN_PARTS):
            if c == 0:
                h[r] = (_layer_norm(x_ref[0, rows[r]]) * (1.0 + m[4:5]) + m[3:4]).astype(BF16)
            gate = jnp.dot(h[r], wg_ref[c], preferred_element_type=F32)
            up = jnp.dot(h[r], wu_ref[c], preferred_element_type=F32)
            act = (gate * jax.nn.sigmoid(gate) * up).astype(BF16)
            down = jnp.dot(act, wd_ref[c], preferred_element_type=F32)
            acc[r] = down if acc[r] is None else acc[r] + down
            if c == n_chunk - 1:
                o_ref[0, rows[r]] = _post_norm(x_ref[0, rows[r]], m[5:6], acc[r],
                                               l2g_ref[...], l2b_ref[...])


def _ffn(x, mod, wg_b, wu_b, wd_b, l2g, l2b):
    bsz, seq, _ = x.shape
    tile = FFN_TILE
    x_spec = pl.BlockSpec((1, tile, D_MODEL), lambda b, i: (b, i, 0))
    return pl.pallas_call(
        _ffn_kernel,
        grid=(bsz, seq // tile),
        in_specs=[x_spec, pl.BlockSpec((1, 6, D_MODEL), lambda b, i: (b, 0, 0)),
                  _resident(wg_b.shape), _resident(wu_b.shape), _resident(wd_b.shape),
                  _resident(l2g.shape), _resident(l2b.shape)],
        out_specs=x_spec,
        out_shape=jax.ShapeDtypeStruct(x.shape, F32),
        compiler_params=pltpu.CompilerParams(
            dimension_semantics=("arbitrary", "arbitrary"), vmem_limit_bytes=VMEM_LIMIT),
        name="ffn",
    )(x, mod, wg_b, wu_b, wd_b, l2g, l2b)


def _rope_tables(seq):
    inv = ROPE_BASE ** (-jnp.arange(ROPE_PAIRS, dtype=F32) / ROPE_PAIRS)
    t = jnp.arange(seq, dtype=jnp.int32)
    ar = (t // GRID_W).astype(F32)[:, None] * inv
    ac = (t % GRID_W).astype(F32)[:, None] * inv
    cos = jnp.concatenate([jnp.cos(ar), jnp.cos(ar), jnp.cos(ac), jnp.cos(ac)], axis=1)
    sin = jnp.concatenate([-jnp.sin(ar), jnp.sin(ar), -jnp.sin(ac), jnp.sin(ac)], axis=1)
    reps = LANES // HEAD_DIM
    return jnp.tile(cos, (1, reps)), jnp.tile(sin, (1, reps))


def kernel(x, c, ctx, c_ctx, w_ada, b_ada, w_in, attn_sink, gmlp_ln_g, gmlp_ln_b, w_spatial, b_spatial,
           w_branch_a, w_branch_b, w_out, ln1_g, ln1_b, w_ffn_in, w_ffn_out, ln2_g, ln2_b):
    bsz, seq, _ = x.shape
    assert seq % MIX_TILE == 0 and seq % FFN_TILE == 0 and seq % GRID_W == 0
    assert w_ada.shape[0] == DEPTH

    pad = (-(bsz + 1)) % 8
    c_rows = jnp.concatenate([c, c_ctx[None], jnp.zeros((pad, D_MODEL), F32)], axis=0)
    mod = _ada_rows(c_rows, w_ada[0], b_ada[0]).reshape(c_rows.shape[0], 6, D_MODEL)

    w_in_b = w_in[0].astype(BF16)
    bs_full = jnp.repeat(b_spatial[0].T, GMLP_GROUP_DIM, axis=1)
    ws_pairs = (w_spatial[0].astype(BF16).reshape(N_GMLP_GROUPS // 2, 2, BLOCK, BLOCK)
                .transpose(0, 2, 1, 3).reshape(N_GMLP_GROUPS // 2, BLOCK, 2 * BLOCK))
    sink_rows = jnp.repeat(attn_sink[0], BLOCK).reshape(N_KV_HEADS, GQA_GROUP * BLOCK)
    row = lambda a: a[0].reshape(1, -1)

    tables = _rope_tables(seq)
    kc, vct = _kv_context(ctx, mod, bsz, w_in_b)

    x_mid, wg_b, wu_b, wd_b = _mixer(
        x, mod, sink_rows, tables, kc, vct, w_in_b, row(gmlp_ln_g), row(gmlp_ln_b), ws_pairs, bs_full,
        w_branch_a[0].astype(BF16), w_branch_b[0].astype(BF16), w_out[0].astype(BF16),
        row(ln1_g), row(ln1_b), w_ffn_in[0], w_ffn_out[0])
    return _ffn(x_mid, mod, wg_b, wu_b, wd_b, row(ln2_g), row(ln2_b))
```
